```python
import math
import jax, jax.numpy as jnp
from jax import lax
import numpy as np

D_MODEL = 2048
BATCH = 2
SEQ = 4096
DEPTH = 4
DEC_BATCH = 8
DEC_SEQ = 64
PAST_LEN = 1024

CHUNK = 64
N_MIXERS = 3
N_A = (DEPTH + 2) // 3
N_B = (DEPTH + 1) // 3
N_C = DEPTH // 3
N_SUB = 3
D_FF = 5632
FFN_RES = 0.5
A_WIDTH = 3
B_WIDTH = 31
N_HEADS = 8
HEAD_DIM = D_MODEL // N_HEADS // 2
V_DIM = 2 * HEAD_DIM
Q_BLOCK = 128
NORM_EPS = 1e-6
SUBLN_EPS = 1e-5
LN_EPS = 1e-5
NEG_INF = -1e30

kernel_name = "chunk_causal_hybrid_conv_diffattn_step"


def alibi_slopes(n_heads):
    return jnp.asarray([2.0 ** (-8.0 * (h + 1) / n_heads) for h in range(n_heads)], dtype=jnp.float32)


def rms_norm(x, g, eps):
    xf = x.astype(jnp.float32)
    y = xf * lax.rsqrt(jnp.mean(xf * xf, axis=-1, keepdims=True) + eps)
    return (y * g.astype(jnp.float32)).astype(x.dtype)


def layer_norm(x, g, b, eps):
    xf = x.astype(jnp.float32)
    mu = jnp.mean(xf, axis=-1, keepdims=True)
    xc = xf - mu
    y = xc * lax.rsqrt(jnp.mean(xc * xc, axis=-1, keepdims=True) + eps)
    return (y * g.astype(jnp.float32) + b.astype(jnp.float32)).astype(x.dtype)


def causal_depthwise(u, hist, w):
    width = w.shape[0]
    ext = jnp.concatenate([hist.astype(u.dtype), u], axis=1)
    y = lax.conv_general_dilated(ext, w[:, None, :].astype(u.dtype), window_strides=(1,),
                                 padding='VALID', dimension_numbers=('NWC', 'WIO', 'NWC'),
                                 feature_group_count=u.shape[-1])
    return y, ext[:, ext.shape[1] - (width - 1):]


def swiglu_ffn(h, w_gu, w_down):
    g, u = jnp.split(h @ w_gu, 2, axis=-1)
    return (jax.nn.silu(g) * u) @ w_down


def short_conv_mixer(h, hist, w_in, w_conv, w_out):
    b, cg, xin = jnp.split(h @ w_in, 3, axis=-1)
    conv, new_hist = causal_depthwise(cg * xin, hist, w_conv)
    return (b * conv) @ w_out, new_hist


def conformer_conv_mixer(h, hist, w_pw1, b_pw1, w_dw, b_dw, ln_g, ln_b, w_pw2, b_pw2):
    a, g = jnp.split(h @ w_pw1 + b_pw1, 2, axis=-1)
    u = a * jax.nn.sigmoid(g)
    conv, new_hist = causal_depthwise(u, hist, w_dw)
    n = layer_norm(conv + b_dw, ln_g, ln_b, LN_EPS)
    return jax.nn.silu(n) @ w_pw2 + b_pw2, new_hist


def diff_attn_core(q, k, v, q_pos, k_pos, lam, slopes):
    s = jnp.einsum('bqhcd,bkhcd->bchqk', q, k,
                   preferred_element_type=jnp.float32) * (HEAD_DIM ** -0.5)
    dist = jnp.abs(q_pos[:, None] - k_pos[None, :]).astype(jnp.float32)
    bias = -slopes[:, None, None] * dist[None]
    visible = (k_pos[None, :] // CHUNK) <= (q_pos[:, None] // CHUNK)
    s = jnp.where(visible[None, None, None], s + bias[None, None], NEG_INF)
    p = jax.nn.softmax(s, axis=-1)
    a = p[:, 0] - lam * p[:, 1]
    return jnp.einsum('bhqk,bkhe->bqhe', a.astype(v.dtype), v)


def diff_attn_mixer(h, hist_k, hist_v, pos0, w_qkv, lam_params, subln_g, w_o, lam_init, slopes):
    B, T, _ = h.shape
    q, k, v = jnp.split(h @ w_qkv, 3, axis=-1)
    q = q.reshape(B, T, N_HEADS, 2, HEAD_DIM)
    k_rows = k.reshape(B, T, N_HEADS, V_DIM)
    v_rows = v.reshape(B, T, N_HEADS, V_DIM)
    lp = lam_params.astype(jnp.float32)
    lam = jnp.exp(jnp.sum(lp[0] * lp[1])) - jnp.exp(jnp.sum(lp[2] * lp[3])) + lam_init
    q_pos = pos0 + jnp.arange(T, dtype=jnp.int32)
    if hist_k is None:
        nb = T // Q_BLOCK
        kk = k_rows.reshape(B, T, N_HEADS, 2, HEAD_DIM)
        qb = jnp.moveaxis(q.reshape(B, nb, Q_BLOCK, N_HEADS, 2, HEAD_DIM), 1, 0)
        pb = q_pos.reshape(nb, Q_BLOCK)
        o = lax.map(lambda qp: diff_attn_core(qp[0], kk, v_rows, qp[1], q_pos, lam, slopes), (qb, pb))
        o = jnp.moveaxis(o, 0, 1).reshape(B, T, N_HEADS, V_DIM)
    else:
        past = hist_k.shape[1]
        k_all = jnp.concatenate([hist_k.astype(k_rows.dtype), k_rows], axis=1)
        v_all = jnp.concatenate([hist_v.astype(v_rows.dtype), v_rows], axis=1)
        k_pos = jnp.arange(past + T, dtype=jnp.int32)
        o = diff_attn_core(q, k_all.reshape(B, past + T, N_HEADS, 2, HEAD_DIM), v_all,
                           q_pos, k_pos, lam, slopes)
    o = rms_norm(o, subln_g, SUBLN_EPS) * (1.0 - lam_init)
    return o.reshape(B, T, D_MODEL) @ w_o, k_rows, v_rows


def modulated_pre(x, g, shift, scale):
    return rms_norm(x, g, NORM_EPS) * (1 + scale[:, None]) + shift[:, None]


def gated_post(x, o, g, gate, rw):
    return x + rw * gate[:, None] * rms_norm(o, g, NORM_EPS)


def trunk(x, c, hist_a, hist_b, hist_k, hist_v, pos0,
          w_mod, b_mod, g_pre, g_post, ffn_w_gu, ffn_w_down,
          sc_w_in, sc_w_conv, sc_w_out,
          cm_w_pw1, cm_b_pw1, cm_w_dw, cm_b_dw, cm_ln_g, cm_ln_b, cm_w_pw2, cm_b_pw2,
          attn_w_qkv, attn_lambda, attn_subln_g, attn_w_o):
    B = x.shape[0]
    slopes = alibi_slopes(N_HEADS)
    new_a, new_b, new_k, new_v = [], [], [], []
    for i in range(DEPTH):
        mod = (jax.nn.silu(c) @ w_mod[i] + b_mod[i]).reshape(B, N_SUB, 3, D_MODEL)
        h = modulated_pre(x, g_pre[i, 0], mod[:, 0, 0], mod[:, 0, 1])
        x = gated_post(x, swiglu_ffn(h, ffn_w_gu[i, 0], ffn_w_down[i, 0]), g_post[i, 0], mod[:, 0, 2], FFN_RES)
        h = modulated_pre(x, g_pre[i, 1], mod[:, 1, 0], mod[:, 1, 1])
        kind, idx = i % N_MIXERS, i // N_MIXERS
        if kind == 0:
            o, st = short_conv_mixer(h, hist_a[idx], sc_w_in[idx], sc_w_conv[idx], sc_w_out[idx])
            new_a.append(st)
        elif kind == 1:
            o, st = conformer_conv_mixer(h, hist_b[idx], cm_w_pw1[idx], cm_b_pw1[idx], cm_w_dw[idx],
                                         cm_b_dw[idx], cm_ln_g[idx], cm_ln_b[idx], cm_w_pw2[idx], cm_b_pw2[idx])
            new_b.append(st)
        else:
            lam_init = 0.8 - 0.6 * math.exp(-0.3 * i)
            hk = None if hist_k is None else hist_k[idx]
            hv = None if hist_v is None else hist_v[idx]
            o, kr, vr = diff_attn_mixer(h, hk, hv, pos0, attn_w_qkv[idx], attn_lambda[idx],
                                        attn_subln_g[idx], attn_w_o[idx], lam_init, slopes)
            new_k.append(kr)
            new_v.append(vr)
        x = gated_post(x, o, g_post[i, 1], mod[:, 1, 2], 1.0)
        h = modulated_pre(x, g_pre[i, 2], mod[:, 2, 0], mod[:, 2, 1])
        x = gated_post(x, swiglu_ffn(h, ffn_w_gu[i, 1], ffn_w_down[i, 1]), g_post[i, 2], mod[:, 2, 2], FFN_RES)
    return x, jnp.stack(new_a), jnp.stack(new_b), jnp.stack(new_k), jnp.stack(new_v)


def setup_inputs(seed: int = 0) -> dict:
    key = jax.random.key(seed)
    ks = iter(jax.random.split(key, 40))
    f32 = jnp.float32
    nrm = lambda shape, s: jax.random.normal(next(ks), shape, f32) * s
    D = D_MODEL
    return {
        "x_prompt": nrm((BATCH, SEQ, D), 1.0),
        "x_sample": nrm((DEC_BATCH, DEC_SEQ, D), 1.0),
        "state_conv_a": nrm((N_A, DEC_BATCH, A_WIDTH - 1, D), 1.0),
        "state_conv_b": nrm((N_B, DEC_BATCH, B_WIDTH - 1, D), 0.6),
        "cache_k": nrm((N_C, DEC_BATCH, PAST_LEN, N_HEADS, V_DIM), 1.0),
        "cache_v": nrm((N_C, DEC_BATCH, PAST_LEN, N_HEADS, V_DIM), 1.0),
        "c_prompt": nrm((BATCH, D), 1.0),
        "c_sample": nrm((DEC_BATCH, D), 1.0),
        "w_mod": nrm((DEPTH, D, N_SUB * 3 * D), 0.5 * D ** -0.5),
        "b_mod": nrm((DEPTH, N_SUB * 3 * D), 0.01),
        "g_pre": 1.0 + nrm((DEPTH, N_SUB, D), 0.05),
        "g_post": 1.0 + nrm((DEPTH, N_SUB, D), 0.05),
        "ffn_w_gu": nrm((DEPTH, 2, D, 2 * D_FF), D ** -0.5),
        "ffn_w_down": nrm((DEPTH, 2, D_FF, D), D_FF ** -0.5),
        "sc_w_in": nrm((N_A, D, 3 * D), D ** -0.5),
        "sc_w_conv": nrm((N_A, A_WIDTH, D), A_WIDTH ** -0.5),
        "sc_w_out": nrm((N_A, D, D), D ** -0.5),
        "cm_w_pw1": nrm((N_B, D, 2 * D), D ** -0.5),
        "cm_b_pw1": nrm((N_B, 2 * D), 0.01),
        "cm_w_dw": nrm((N_B, B_WIDTH, D), B_WIDTH ** -0.5),
        "cm_b_dw": nrm((N_B, D), 0.01),
        "cm_ln_g": 1.0 + nrm((N_B, D), 0.05),
        "cm_ln_b": nrm((N_B, D), 0.01),
        "cm_w_pw2": nrm((N_B, D, D), D ** -0.5),
        "cm_b_pw2": nrm((N_B, D), 0.01),
        "attn_w_qkv": nrm((N_C, D, 3 * D), D ** -0.5),
        "attn_lambda": nrm((N_C, 4, HEAD_DIM), 0.1),
        "attn_subln_g": 1.0 + nrm((N_C, V_DIM), 0.05),
        "attn_w_o": nrm((N_C, D, D), D ** -0.5),
    }


def reference(x_prompt, x_sample, state_conv_a, state_conv_b, cache_k, cache_v, c_prompt, c_sample,
              w_mod, b_mod, g_pre, g_post, ffn_w_gu, ffn_w_down,
              sc_w_in, sc_w_conv, sc_w_out,
              cm_w_pw1, cm_b_pw1, cm_w_dw, cm_b_dw, cm_ln_g, cm_ln_b, cm_w_pw2, cm_b_pw2,
              attn_w_qkv, attn_lambda, attn_subln_g, attn_w_o):
    Bp = x_prompt.shape[0]
    zeros_a = jnp.zeros((N_A, Bp, A_WIDTH - 1, D_MODEL), x_prompt.dtype)
    zeros_b = jnp.zeros((N_B, Bp, B_WIDTH - 1, D_MODEL), x_prompt.dtype)
    y_prompt, conv_a_prompt, conv_b_prompt, k_prompt, v_prompt = trunk(
        x_prompt, c_prompt, zeros_a, zeros_b, None, None, 0,
        w_mod, b_mod, g_pre, g_post, ffn_w_gu, ffn_w_down,
        sc_w_in, sc_w_conv, sc_w_out,
        cm_w_pw1, cm_b_pw1, cm_w_dw, cm_b_dw, cm_ln_g, cm_ln_b, cm_w_pw2, cm_b_pw2,
        attn_w_qkv, attn_lambda, attn_subln_g, attn_w_o)
    past = cache_k.shape[2]
    y_sample, conv_a_sample, conv_b_sample, k_sample, v_sample = trunk(
        x_sample, c_sample, state_conv_a, state_conv_b, cache_k, cache_v, past,
        w_mod, b_mod, g_pre, g_post, ffn_w_gu, ffn_w_down,
        sc_w_in, sc_w_conv, sc_w_out,
        cm_w_pw1, cm_b_pw1, cm_w_dw, cm_b_dw, cm_ln_g, cm_ln_b, cm_w_pw2, cm_b_pw2,
        attn_w_qkv, attn_lambda, attn_subln_g, attn_w_o)
    return (y_prompt, y_sample, conv_a_prompt, conv_a_sample, conv_b_prompt, conv_b_sample,
            k_prompt, v_prompt, k_sample, v_sample)
```

```python
import functools
import math

import jax
import jax.numpy as jnp
from jax import lax
from jax.experimental import pallas as pl
from jax.experimental.pallas import tpu as pltpu

F32 = jnp.float32
BF16 = jnp.bfloat16

N_HEADS = 8
CHUNK = 64
FFN_RES = 0.5
NORM_EPS = 1e-6
SUBLN_EPS = 1e-5
LN_EPS = 1e-5
NEG_INF = -1e30
N_SUB = 3
N_MIXERS = 3

SEQ_PAD = 16
V7X_VMEM_LIMIT_BYTES = 56 * 1024 * 1024
SUBLANES = 8

FFN_ROWS = 1024
FFN_FF_TILE = 512
FFN_DOWN_TILE = 512
MM_ROWS = 512
PRE_N_TILE = 1024
CONV_ROWS = 512
CONV_COLS = 256
CONV_SUB_ROWS = 64
ATTN_Q_TILE = 512
ATTN_K_TILE = 512
LANES = 128


def _cparams(sem):
    return pltpu.CompilerParams(dimension_semantics=sem, vmem_limit_bytes=V7X_VMEM_LIMIT_BYTES)


def _row_blocks(G, R, rows):
    if R >= rows:
        assert R % rows == 0
        return 1, rows
    gb = min(G, rows // R)
    assert G % gb == 0
    return gb, R


def _rms(x, g, eps):
    return x * lax.rsqrt(jnp.mean(x * x, axis=-1, keepdims=True) + eps) * g


def _silu(x):
    return x * jax.nn.sigmoid(x)


def _dot(a, b):
    return jnp.dot(a, b, preferred_element_type=F32)


def _mod_row(ref, seq0, b):
    return ref[pl.ds(seq0 + b, 1), :]


def _adaln_body(c_ref, w_ref, b_ref, o_ref):
    a = _silu(c_ref[...]).astype(BF16)
    o_ref[...] = _dot(a, w_ref[...].astype(BF16)) + b_ref[...]


def _adaln_mod(c_all, w_mod, b_mod):
    L, D, N = w_mod.shape
    nm = N // D
    P = c_all.shape[0]
    return pl.pallas_call(
        _adaln_body,
        grid=(L, nm),
        in_specs=[pl.BlockSpec((P, D), lambda l, m: (0, 0)),
                  pl.BlockSpec((None, D, D), lambda l, m: (l, 0, m)),
                  pl.BlockSpec((None, 1, D), lambda l, m: (l * nm + m, 0, 0))],
        out_specs=pl.BlockSpec((None, P, D), lambda l, m: (l * nm + m, 0, 0)),
        out_shape=jax.ShapeDtypeStruct((L * nm, P, D), F32),
        compiler_params=_cparams(("arbitrary", "arbitrary")),
        name="adaln_mod",
    )(c_all, w_mod, b_mod.reshape(L * nm, 1, D))


def _mod_specs(layer, sub, kinds, D):
    specs = []
    for k in kinds:
        idx = layer * 3 * N_SUB + 3 * sub + k
        specs.append(pl.BlockSpec((None, SEQ_PAD, D), lambda *_, idx=idx: (idx, 0, 0)))
    return specs


def _ffn_body(x_ref, sh_ref, sc_ref, gt_ref, gpre_ref, gpost_ref, wg_ref, wu_ref, wd_ref,
              o_ref, h_ref, *, row0, n_chunks):
    j = pl.program_id(2)
    gb, rb, _ = x_ref.shape
    seq0 = row0 + pl.program_id(0) * gb

    @pl.when(j == 0)
    def _():
        for b in range(gb):
            h = (_rms(x_ref[b], gpre_ref[...], NORM_EPS) * (1.0 + _mod_row(sc_ref, seq0, b))
                 + _mod_row(sh_ref, seq0, b))
            h_ref[b * rb:(b + 1) * rb, :] = h.astype(BF16)
        o_ref[...] = jnp.zeros(o_ref.shape, F32)

    h = h_ref[...]
    g = _dot(h, wg_ref[...])
    u = _dot(h, wu_ref[...])
    act = (_silu(g) * u).astype(BF16)
    dn = min(FFN_DOWN_TILE, o_ref.shape[2])
    for n0 in range(0, o_ref.shape[2], dn):
        y = _dot(act, wd_ref[:, n0:n0 + dn])
        for b in range(gb):
            o_ref[b, :, n0:n0 + dn] += y[b * rb:(b + 1) * rb, :]

    @pl.when(j == n_chunks - 1)
    def _():
        for b in range(gb):
            o_ref[b] = x_ref[b] + FFN_RES * _mod_row(gt_ref, seq0, b) * _rms(
                o_ref[b], gpost_ref[...], NORM_EPS)


def _ffn(x, mod, row0, layer, sub, ffn_idx, g_pre, g_post, w_gu, w_down):
    G, R, D = x.shape
    dff = w_down.shape[2]
    tf = FFN_FF_TILE
    n_chunks = dff // tf
    gb, rb = _row_blocks(G, R, FFN_ROWS)
    gidx = layer * N_SUB + sub
    body = functools.partial(_ffn_body, row0=row0, n_chunks=n_chunks)
    xspec = pl.BlockSpec((gb, rb, D), lambda g, t, j: (g, t, 0))
    xin_spec = pl.BlockSpec((gb, rb, D), lambda g, t, j: (g, t, 0), pipeline_mode=pl.Buffered(1))
    return pl.pallas_call(
        body,
        grid=(G // gb, R // rb, n_chunks),
        in_specs=[xin_spec] + _mod_specs(layer, sub, (0, 1, 2), D) + [
            pl.BlockSpec((None, 1, D), lambda g, t, j: (gidx, 0, 0)),
            pl.BlockSpec((None, 1, D), lambda g, t, j: (gidx, 0, 0)),
            pl.BlockSpec((None, None, D, tf), lambda g, t, j: (layer, ffn_idx, 0, j)),
            pl.BlockSpec((None, None, D, tf), lambda g, t, j: (layer, ffn_idx, 0, j + n_chunks)),
            pl.BlockSpec((None, None, tf, D), lambda g, t, j: (layer, ffn_idx, j, 0)),
        ],
        out_specs=xspec,
        out_shape=jax.ShapeDtypeStruct((G, R, D), F32),
        scratch_shapes=[pltpu.VMEM((gb * rb, D), BF16)],
        compiler_params=_cparams(("arbitrary", "arbitrary", "arbitrary")),
        name="ffn",
    )(x, mod, mod, mod, g_pre, g_post, w_gu, w_gu, w_down)


def _pre_mm_body(x_ref, sh_ref, sc_ref, gpre_ref, w_ref, *rest, row0, has_bias):
    if has_bias:
        b_ref, o_ref, h_ref = rest
    else:
        o_ref, h_ref = rest
    gb, rb, _ = x_ref.shape
    seq0 = row0 + pl.program_id(0) * gb

    @pl.when(pl.program_id(2) == 0)
    def _():
        for b in range(gb):
            h = (_rms(x_ref[b], gpre_ref[...], NORM_EPS) * (1.0 + _mod_row(sc_ref, seq0, b))
                 + _mod_row(sh_ref, seq0, b))
            h_ref[b * rb:(b + 1) * rb, :] = h.astype(BF16)

    z = _dot(h_ref[...], w_ref[...])
    if has_bias:
        z = z + b_ref[...]
    for b in range(gb):
        o_ref[b] = z[b * rb:(b + 1) * rb, :]


def _pre_mm(x, mod, row0, layer, g_pre, w, widx, bias=None):
    G, R, D = x.shape
    N = w.shape[2]
    tn = PRE_N_TILE
    gb, rb = _row_blocks(G, R, MM_ROWS)
    gidx = layer * N_SUB + 1
    has_bias = bias is not None
    body = functools.partial(_pre_mm_body, row0=row0, has_bias=has_bias)
    in_specs = [pl.BlockSpec((gb, rb, D), lambda g, t, n: (g, t, 0))] + _mod_specs(layer, 1, (0, 1), D) + [
        pl.BlockSpec((None, 1, D), lambda g, t, n: (gidx, 0, 0)),
        pl.BlockSpec((None, D, tn), lambda g, t, n: (widx, 0, n)),
    ]
    args = [x, mod, mod, g_pre, w]
    if has_bias:
        in_specs.append(pl.BlockSpec((None, 1, tn), lambda g, t, n: (widx, 0, n)))
        args.append(bias.reshape(bias.shape[0], 1, N))
    return pl.pallas_call(
        body,
        grid=(G // gb, R // rb, N // tn),
        in_specs=in_specs,
        out_specs=pl.BlockSpec((gb, rb, tn), lambda g, t, n: (g, t, n)),
        out_shape=jax.ShapeDtypeStruct((G, R, N), F32),
        scratch_shapes=[pltpu.VMEM((gb * rb, D), BF16)],
        compiler_params=_cparams(("arbitrary", "arbitrary", "arbitrary")),
        name="pre_mm",
    )(*args)


def _mm_post_body(y_ref, w_ref, x_ref, gt_ref, gpost_ref, *rest, row0, has_bias, has_ln):
    rest = list(rest)
    b_ref = rest.pop(0) if has_bias else None
    lng_ref, lnb_ref = (rest.pop(0), rest.pop(0)) if has_ln else (None, None)
    o_ref, a_ref = rest
    gb, rb, _ = x_ref.shape
    seq0 = row0 + pl.program_id(0) * gb

    for b in range(gb):
        y = y_ref[b]
        if has_ln:
            mu = jnp.mean(y, axis=-1, keepdims=True)
            yc = y - mu
            y = _silu(yc * lax.rsqrt(jnp.mean(yc * yc, axis=-1, keepdims=True) + LN_EPS) * lng_ref[...]
                      + lnb_ref[...])
        a_ref[b * rb:(b + 1) * rb, :] = y.astype(BF16)

    o = _dot(a_ref[...], w_ref[...])
    if has_bias:
        o = o + b_ref[...]
    for b in range(gb):
        o_ref[b] = x_ref[b] + _mod_row(gt_ref, seq0, b) * _rms(
            o[b * rb:(b + 1) * rb, :], gpost_ref[...], NORM_EPS)


def _mm_post(y, x, mod, row0, layer, g_post, w, widx, bias=None, ln=None):
    G, R, D = x.shape
    Din = y.shape[2]
    gb, rb = _row_blocks(G, R, MM_ROWS)
    gidx = layer * N_SUB + 1
    has_bias, has_ln = bias is not None, ln is not None
    body = functools.partial(_mm_post_body, row0=row0, has_bias=has_bias, has_ln=has_ln)
    xspec = pl.BlockSpec((gb, rb, D), lambda g, t: (g, t, 0))
    in_specs = [pl.BlockSpec((gb, rb, Din), lambda g, t: (g, t, 0)),
                pl.BlockSpec((None, Din, D), lambda g, t: (widx, 0, 0)),
                xspec] + _mod_specs(layer, 1, (2,), D) + [
        pl.BlockSpec((None, 1, D), lambda g, t: (gidx, 0, 0))]
    args = [y, w, x, mod, g_post]
    if has_bias:
        in_specs.append(pl.BlockSpec((None, 1, D), lambda g, t: (widx, 0, 0)))
        args.append(bias.reshape(bias.shape[0], 1, D))
    if has_ln:
        for p in ln:
            in_specs.append(pl.BlockSpec((None, 1, Din), lambda g, t: (widx, 0, 0)))
            args.append(p.reshape(p.shape[0], 1, Din))
    return pl.pallas_call(
        body,
        grid=(G // gb, R // rb),
        in_specs=in_specs,
        out_specs=xspec,
        out_shape=jax.ShapeDtypeStruct((G, R, D), F32),
        scratch_shapes=[pltpu.VMEM((gb * rb, Din), BF16)],
        compiler_params=_cparams(("arbitrary", "arbitrary")),
        name="mm_post",
    )(*args)


def _conv_core(ext_ref, w_ref, b, t, nt, hist_ref, nh_ref, new_rows, emit):
    W = w_ref.shape[0]
    rb = new_rows.shape[0]
    pad = ext_ref.shape[1] - rb
    lo = pad - (W - 1)

    @pl.when(t == 0)
    def _():
        ext_ref[b, lo:pad, :] = hist_ref[b]

    @pl.when(t > 0)
    def _():
        ext_ref[b, lo:pad, :] = ext_ref[b, lo + rb:pad + rb, :]

    ext_ref[b, pad:pad + rb, :] = new_rows
    rs = min(CONV_SUB_ROWS, rb)
    for r0 in range(0, rb, rs):
        acc = w_ref[0:1, :] * ext_ref[b, lo + r0:lo + r0 + rs, :]
        for j in range(1, W):
            acc = acc + w_ref[j:j + 1, :] * ext_ref[b, lo + j + r0:lo + j + r0 + rs, :]
        emit(r0, rs, acc)

    @pl.when(t == nt - 1)
    def _():
        nh_ref[b] = ext_ref[b, lo + rb:pad + rb, :]


def _conv_a_body(zb_ref, zc_ref, zx_ref, hist_ref, w_ref, y_ref, nh_ref, ext_ref, *, nt):
    t = pl.program_id(2)
    for b in range(zb_ref.shape[0]):
        def emit(r0, rs, acc, b=b):
            y_ref[b, r0:r0 + rs, :] = (zb_ref[b, r0:r0 + rs, :] * acc).astype(y_ref.dtype)
        _conv_core(ext_ref, w_ref, b, t, nt, hist_ref, nh_ref, zc_ref[b] * zx_ref[b], emit)


def _conv_b_body(za_ref, zg_ref, hist_ref, w_ref, bdw_ref, y_ref, nh_ref, ext_ref, *, nt):
    t = pl.program_id(2)
    for b in range(za_ref.shape[0]):
        def emit(r0, rs, acc, b=b):
            y_ref[b, r0:r0 + rs, :] = acc + bdw_ref[...]
        _conv_core(ext_ref, w_ref, b, t, nt, hist_ref, nh_ref,
                   za_ref[b] * jax.nn.sigmoid(zg_ref[b]), emit)


def _conv(kind, z, hist, w, widx, bdw=None):
    G, R, N = z.shape
    nsplit = 3 if kind == "a" else 2
    D = N // nsplit
    W = w.shape[1]
    gb, rb = _row_blocks(G, R, CONV_ROWS)
    assert rb >= W - 1
    tc = CONV_COLS
    nc = D // tc
    nt = R // rb
    pad = -(-(W - 1) // SUBLANES) * SUBLANES
    zspecs = [pl.BlockSpec((gb, rb, tc), lambda g, c, t, s=s: (g, t, s * nc + c)) for s in range(nsplit)]
    in_specs = zspecs + [pl.BlockSpec((gb, W - 1, tc), lambda g, c, t: (g, 0, c)),
                         pl.BlockSpec((None, W, tc), lambda g, c, t: (widx, 0, c))]
    args = [z] * nsplit + [hist, w]
    if kind == "a":
        body, out_dtype = _conv_a_body, BF16
    else:
        body, out_dtype = _conv_b_body, F32
        in_specs.append(pl.BlockSpec((None, 1, tc), lambda g, c, t: (widx, 0, c)))
        args.append(bdw.reshape(bdw.shape[0], 1, D))
    return pl.pallas_call(
        functools.partial(body, nt=nt),
        grid=(G // gb, nc, nt),
        in_specs=in_specs,
        out_specs=[pl.BlockSpec((gb, rb, tc), lambda g, c, t: (g, t, c)),
                   pl.BlockSpec((gb, W - 1, tc), lambda g, c, t: (g, 0, c))],
        out_shape=[jax.ShapeDtypeStruct((G, R, D), out_dtype),
                   jax.ShapeDtypeStruct((G, W - 1, D), F32)],
        scratch_shapes=[pltpu.VMEM((gb, pad + rb, tc), F32)],
        compiler_params=_cparams(("arbitrary", "arbitrary", "arbitrary")),
        name="conv_" + kind,
    )(*args)


def _last_visible_tile(q_pos, tk):
    return ((q_pos // CHUNK) * CHUNK + CHUNK - 1) // tk


def _attn_body(slope_ref, lam_ref, subg_ref, q_ref, k_ref, v_ref, o_ref, m_ref, l_ref, acc_ref,
               *, pos0, n_keys, nk, lam_init):
    h, qi, ki = pl.program_id(1), pl.program_id(2), pl.program_id(3)
    tq, dh2 = q_ref.shape
    tk = k_ref.shape[0]
    dh = dh2 // 2
    shift = int(math.log2(CHUNK))

    @pl.when(ki == 0)
    def _():
        m_ref[...] = jnp.full(m_ref.shape, NEG_INF, F32)
        l_ref[...] = jnp.zeros(l_ref.shape, F32)
        acc_ref[...] = jnp.zeros(acc_ref.shape, F32)

    q_lo = pos0 + qi * tq

    @pl.when(ki <= _last_visible_tile(q_lo + tq - 1, tk))
    def _():
        qpos = q_lo + lax.broadcasted_iota(jnp.int32, (tq, 1), 0)
        kpos = ki * tk + lax.broadcasted_iota(jnp.int32, (1, tk), 1)
        visible = (jnp.right_shift(kpos, shift) <= jnp.right_shift(qpos, shift)) & (kpos < n_keys)
        bias = -slope_ref[h] * jnp.abs(qpos - kpos).astype(F32)
        v = v_ref[...].astype(BF16)
        for c in range(2):
            qc = q_ref[:, c * dh:(c + 1) * dh].astype(BF16)
            kc = k_ref[:, c * dh:(c + 1) * dh].astype(BF16)
            s = lax.dot_general(qc, kc, (((1,), (1,)), ((), ())), preferred_element_type=F32) * (dh ** -0.5)
            s = jnp.where(visible, s + bias, NEG_INF)
            m_old = m_ref[c]
            m_new = jnp.maximum(m_old, jnp.max(s, axis=-1, keepdims=True))
            alpha = jnp.exp(m_old - m_new)
            p = jnp.exp(s - m_new)
            l_ref[c] = alpha * l_ref[c] + jnp.sum(p, axis=-1, keepdims=True)
            acc_ref[c] = alpha * acc_ref[c] + _dot(p.astype(BF16), v)
            m_ref[c] = m_new

    @pl.when(ki == nk - 1)
    def _():
        lp = lam_ref[...]
        lam = (jnp.exp(jnp.sum(lp[0:1] * lp[1:2], axis=-1, keepdims=True))
               - jnp.exp(jnp.sum(lp[2:3] * lp[3:4], axis=-1, keepdims=True)) + lam_init)
        o = acc_ref[0] / l_ref[0] - lam * (acc_ref[1] / l_ref[1])
        o_ref[...] = (_rms(o, subg_ref[...], SUBLN_EPS) * (1.0 - lam_init)).astype(o_ref.dtype)


def _diff_attn(q_arr, q_off, k_arr, k_off, v_arr, v_off, n_keys, pos0, lam_params, subln_g, idx, lam_init,
               slopes):
    B, Tq, _ = q_arr.shape
    Tk = k_arr.shape[1]
    dh2 = subln_g.shape[1]
    H = N_HEADS
    tq = min(ATTN_Q_TILE, Tq)
    tk = Tk if Tk <= 2 * ATTN_K_TILE + LANES else ATTN_K_TILE
    assert Tq % tq == 0 and Tk % tk == 0
    nq, nk = Tq // tq, Tk // tk
    assert 1 << int(math.log2(CHUNK)) == CHUNK

    def k_index(off):
        def index(b, h, qi, ki):
            last = jnp.minimum(_last_visible_tile(pos0 + (qi + 1) * tq - 1, tk), nk - 1)
            return (b, jnp.minimum(ki, last), off + h)
        return index

    body = functools.partial(_attn_body, pos0=pos0, n_keys=n_keys, nk=nk, lam_init=lam_init)
    return pl.pallas_call(
        body,
        grid=(B, H, nq, nk),
        in_specs=[pl.BlockSpec(memory_space=pltpu.SMEM),
                  pl.BlockSpec((None, 4, dh2 // 2), lambda b, h, qi, ki: (idx, 0, 0)),
                  pl.BlockSpec((None, 1, dh2), lambda b, h, qi, ki: (idx, 0, 0)),
                  pl.BlockSpec((None, tq, dh2), lambda b, h, qi, ki: (b, qi, q_off + h)),
                  pl.BlockSpec((None, tk, dh2), k_index(k_off)),
                  pl.BlockSpec((None, tk, dh2), k_index(v_off))],
        out_specs=pl.BlockSpec((None, tq, dh2), lambda b, h, qi, ki: (b, qi, h)),
        out_shape=jax.ShapeDtypeStruct((B, Tq, H * dh2), BF16),
        scratch_shapes=[pltpu.VMEM((2, tq, 1), F32), pltpu.VMEM((2, tq, 1), F32),
                        pltpu.VMEM((2, tq, dh2), F32)],
        compiler_params=_cparams(("arbitrary", "arbitrary", "arbitrary", "arbitrary")),
        name="diff_attn",
    )(slopes, lam_params, subln_g.reshape(subln_g.shape[0], 1, dh2), q_arr, k_arr, v_arr)


def _trunk(x, row0, mod, hist_a, hist_b, hist_k, hist_v, pos0, p):
    G, R, D = x.shape
    depth = p["ffn_w_gu"].shape[0]
    slopes = jnp.asarray([2.0 ** (-8.0 * (h + 1) / N_HEADS) for h in range(N_HEADS)], dtype=F32)
    new_a, new_b, new_k, new_v = [], [], [], []
    for i in range(depth):
        x = _ffn(x, mod, row0, i, 0, 0, p["g_pre"], p["g_post"], p["ffn_w_gu"], p["ffn_w_down"])
        kind, idx = i % N_MIXERS, i // N_MIXERS
        if kind == 0:
            z = _pre_mm(x, mod, row0, i, p["g_pre"], p["sc_w_in"], idx)
            y, st = _conv("a", z, hist_a[idx], p["sc_w_conv"], idx)
            new_a.append(st)
            x = _mm_post(y, x, mod, row0, i, p["g_post"], p["sc_w_out"], idx)
        elif kind == 1:
            z = _pre_mm(x, mod, row0, i, p["g_pre"], p["cm_w_pw1"], idx, bias=p["cm_b_pw1"])
            y, st = _conv("b", z, hist_b[idx], p["cm_w_dw"], idx, bdw=p["cm_b_dw"])
            new_b.append(st)
            x = _mm_post(y, x, mod, row0, i, p["g_post"], p["cm_w_pw2"], idx, bias=p["cm_b_pw2"],
                         ln=(p["cm_ln_g"], p["cm_ln_b"]))
        else:
            lam_init = 0.8 - 0.6 * math.exp(-0.3 * i)
            z = _pre_mm(x, mod, row0, i, p["g_pre"], p["attn_w_qkv"], idx)
            k_rows, v_rows = z[:, :, D:2 * D], z[:, :, 2 * D:]
            new_k.append(k_rows)
            new_v.append(v_rows)
            nblk = D // p["attn_subln_g"].shape[1]
            if hist_k is None:
                o = _diff_attn(z, 0, z, nblk, z, 2 * nblk, R, pos0, p["attn_lambda"], p["attn_subln_g"],
                               idx, lam_init, slopes)
            else:
                past = hist_k.shape[2]
                n_keys = past + R
                padn = -n_keys % LANES
                zpad = jnp.zeros((G, padn, D), F32)
                k_all = jnp.concatenate([hist_k[idx].reshape(G, past, D), k_rows, zpad], axis=1)
                v_all = jnp.concatenate([hist_v[idx].reshape(G, past, D), v_rows, zpad], axis=1)
                o = _diff_attn(z, 0, k_all, 0, v_all, 0, n_keys, pos0, p["attn_lambda"], p["attn_subln_g"],
                               idx, lam_init, slopes)
            x = _mm_post(o, x, mod, row0, i, p["g_post"], p["attn_w_o"], idx)
        x = _ffn(x, mod, row0, i, 2, 1, p["g_pre"], p["g_post"], p["ffn_w_gu"], p["ffn_w_down"])
    return x, jnp.stack(new_a), jnp.stack(new_b), jnp.stack(new_k), jnp.stack(new_v)


def kernel(x_prompt, x_sample, state_conv_a, state_conv_b, cache_k, cache_v, c_prompt, c_sample, w_mod, b_mod, g_pre, g_post, ffn_w_gu, ffn_w_down, sc_w_in, sc_w_conv, sc_w_out, cm_w_pw1, cm_b_pw1, cm_w_dw, cm_b_dw, cm_ln_g, cm_ln_b, cm_w_pw2, cm_b_pw2, attn_w_qkv, attn_lambda, attn_subln_g, attn_w_o):
    Bp, T, D = x_prompt.shape
    Bs = x_sample.shape[0]
    depth = g_pre.shape[0]
    n_a, n_b = state_conv_a.shape[0], state_conv_b.shape[0]
    assert Bp + Bs <= SEQ_PAD
    H, dh2 = N_HEADS, attn_subln_g.shape[1]

    c_all = jnp.concatenate([c_prompt, c_sample, jnp.zeros((SEQ_PAD - Bp - Bs, D), F32)], axis=0)
    mod = _adaln_mod(c_all, w_mod, b_mod)

    p = dict(
        g_pre=g_pre.reshape(depth * N_SUB, 1, D), g_post=g_post.reshape(depth * N_SUB, 1, D),
        ffn_w_gu=ffn_w_gu.astype(BF16), ffn_w_down=ffn_w_down.astype(BF16),
        sc_w_in=sc_w_in.astype(BF16), sc_w_conv=sc_w_conv, sc_w_out=sc_w_out.astype(BF16),
        cm_w_pw1=cm_w_pw1.astype(BF16), cm_b_pw1=cm_b_pw1, cm_w_dw=cm_w_dw, cm_b_dw=cm_b_dw,
        cm_ln_g=cm_ln_g, cm_ln_b=cm_ln_b, cm_w_pw2=cm_w_pw2.astype(BF16), cm_b_pw2=cm_b_pw2,
        attn_w_qkv=attn_w_qkv.astype(BF16), attn_lambda=attn_lambda, attn_subln_g=attn_subln_g,
        attn_w_o=attn_w_o.astype(BF16))

    zeros_a = jnp.zeros((n_a, Bp) + state_conv_a.shape[2:], F32)
    zeros_b = jnp.zeros((n_b, Bp) + state_conv_b.shape[2:], F32)
    y_p, a_p, b_p, k_p, v_p = _trunk(x_prompt, 0, mod, zeros_a, zeros_b, None, None, 0, p)
    past = cache_k.shape[2]
    y_s, a_s, b_s, k_s, v_s = _trunk(x_sample, Bp, mod, state_conv_a, state_conv_b, cache_k, cache_v, past, p)

    def heads(a):
        return a.reshape(a.shape[:3] + (H, dh2))
    return (y_p, y_s, a_p, a_s, b_p, b_s, heads(k_p), heads(v_p), heads(k_s), heads(v_s))
```

```python
import functools
import math

import jax
import jax.numpy as jnp
from jax import lax
from jax.experimental import pallas as pl
from jax.experimental.pallas import tpu as pltpu

F32 = jnp.float32
BF16 = jnp.bfloat16

N_HEADS = 8
CHUNK = 64
FFN_RES = 0.5
NORM_EPS = 1e-6
SUBLN_EPS = 1e-5
LN_EPS = 1e-5
NEG_INF = -1e30
LOG2E = math.log2(math.e)
N_SUB = 3
N_MIXERS = 3

SEQ_PAD = 16
V7X_VMEM_LIMIT_BYTES = 56 * 1024 * 1024
SUBLANES = 8

FFN_ROWS = 1024
FFN_FF_TILE = 512
FFN_DOWN_TILE = 512
MM_ROWS = 512
PRE_N_TILE = 1024
CONV_ROWS = 512
CONV_COLS = 256
CONV_SUB_ROWS = 64
ATTN_Q_TILE = 512
ATTN_K_TILE = 512
LANES = 128


def _cparams(sem):
    return pltpu.CompilerParams(dimension_semantics=sem, vmem_limit_bytes=V7X_VMEM_LIMIT_BYTES)


def _row_blocks(G, R, rows):
    if R >= rows:
        assert R % rows == 0
        return 1, rows
    gb = min(G, rows // R)
    assert G % gb == 0
    return gb, R


def _rms(x, g, eps):
    return x * lax.rsqrt(jnp.mean(x * x, axis=-1, keepdims=True) + eps) * g


def _silu(x):
    return x * jax.nn.sigmoid(x)


def _dot(a, b):
    return jnp.dot(a, b, preferred_element_type=F32)


def _mod_row(ref, seq0, b):
    return ref[pl.ds(seq0 + b, 1), :]


def _adaln_body(c_ref, w_ref, b_ref, o_ref):
    a = _silu(c_ref[...]).astype(BF16)
    o_ref[...] = _dot(a, w_ref[...].astype(BF16)) + b_ref[...]


def _adaln_mod(c_all, w_mod, b_mod):
    L, D, N = w_mod.shape
    nm = N // D
    P = c_all.shape[0]
    return pl.pallas_call(
        _adaln_body,
        grid=(L, nm),
        in_specs=[pl.BlockSpec((P, D), lambda l, m: (0, 0)),
                  pl.BlockSpec((None, D, D), lambda l, m: (l, 0, m)),
                  pl.BlockSpec((None, 1, D), lambda l, m: (l * nm + m, 0, 0))],
        out_specs=pl.BlockSpec((None, P, D), lambda l, m: (l * nm + m, 0, 0)),
        out_shape=jax.ShapeDtypeStruct((L * nm, P, D), F32),
        compiler_params=_cparams(("arbitrary", "arbitrary")),
        name="adaln_mod",
    )(c_all, w_mod, b_mod.reshape(L * nm, 1, D))


def _mod_specs(layer, sub, kinds, D):
    specs = []
    for k in kinds:
        idx = layer * 3 * N_SUB + 3 * sub + k
        specs.append(pl.BlockSpec((None, SEQ_PAD, D), lambda *_, idx=idx: (idx, 0, 0)))
    return specs


def _ffn_body(x_ref, sh_ref, sc_ref, gt_ref, gpre_ref, gpost_ref, wg_ref, wu_ref, wd_ref,
              o_ref, h_ref, *, row0, n_chunks):
    j = pl.program_id(2)
    gb, rb, _ = x_ref.shape
    seq0 = row0 + pl.program_id(0) * gb

    @pl.when(j == 0)
    def _():
        for b in range(gb):
            h = (_rms(x_ref[b], gpre_ref[...], NORM_EPS) * (1.0 + _mod_row(sc_ref, seq0, b))
                 + _mod_row(sh_ref, seq0, b))
            h_ref[b * rb:(b + 1) * rb, :] = h.astype(BF16)
        o_ref[...] = jnp.zeros(o_ref.shape, F32)

    h = h_ref[...]
    g = _dot(h, wg_ref[...])
    u = _dot(h, wu_ref[...])
    act = (_silu(g) * u).astype(BF16)
    dn = min(FFN_DOWN_TILE, o_ref.shape[2])
    for n0 in range(0, o_ref.shape[2], dn):
        y = _dot(act, wd_ref[:, n0:n0 + dn])
        for b in range(gb):
            o_ref[b, :, n0:n0 + dn] += y[b * rb:(b + 1) * rb, :]

    @pl.when(j == n_chunks - 1)
    def _():
        for b in range(gb):
            o_ref[b] = x_ref[b] + FFN_RES * _mod_row(gt_ref, seq0, b) * _rms(
                o_ref[b], gpost_ref[...], NORM_EPS)


def _ffn(x, mod, row0, layer, sub, ffn_idx, g_pre, g_post, w_gu, w_down):
    G, R, D = x.shape
    dff = w_down.shape[2]
    tf = FFN_FF_TILE
    n_chunks = dff // tf
    gb, rb = _row_blocks(G, R, FFN_ROWS)
    gidx = layer * N_SUB + sub
    body = functools.partial(_ffn_body, row0=row0, n_chunks=n_chunks)
    xspec = pl.BlockSpec((gb, rb, D), lambda g, t, j: (g, t, 0))
    xin_spec = pl.BlockSpec((gb, rb, D), lambda g, t, j: (g, t, 0), pipeline_mode=pl.Buffered(1))
    return pl.pallas_call(
        body,
        grid=(G // gb, R // rb, n_chunks),
        in_specs=[xin_spec] + _mod_specs(layer, sub, (0, 1, 2), D) + [
            pl.BlockSpec((None, 1, D), lambda g, t, j: (gidx, 0, 0)),
            pl.BlockSpec((None, 1, D), lambda g, t, j: (gidx, 0, 0)),
            pl.BlockSpec((None, None, D, tf), lambda g, t, j: (layer, ffn_idx, 0, j)),
            pl.BlockSpec((None, None, D, tf), lambda g, t, j: (layer, ffn_idx, 0, j + n_chunks)),
            pl.BlockSpec((None, None, tf, D), lambda g, t, j: (layer, ffn_idx, j, 0)),
        ],
        out_specs=xspec,
        out_shape=jax.ShapeDtypeStruct((G, R, D), F32),
        scratch_shapes=[pltpu.VMEM((gb * rb, D), BF16)],
        compiler_params=_cparams(("arbitrary", "arbitrary", "arbitrary")),
        name="ffn",
    )(x, mod, mod, mod, g_pre, g_post, w_gu, w_gu, w_down)


def _pre_mm_body(x_ref, sh_ref, sc_ref, gpre_ref, w_ref, *rest, row0, has_bias):
    if has_bias:
        b_ref, o_ref, h_ref = rest
    else:
        o_ref, h_ref = rest
    gb, rb, _ = x_ref.shape
    seq0 = row0 + pl.program_id(0) * gb

    @pl.when(pl.program_id(2) == 0)
    def _():
        for b in range(gb):
            h = (_rms(x_ref[b], gpre_ref[...], NORM_EPS) * (1.0 + _mod_row(sc_ref, seq0, b))
                 + _mod_row(sh_ref, seq0, b))
            h_ref[b * rb:(b + 1) * rb, :] = h.astype(BF16)

    z = _dot(h_ref[...], w_ref[...])
    if has_bias:
        z = z + b_ref[...]
    for b in range(gb):
        o_ref[b] = z[b * rb:(b + 1) * rb, :]


def _pre_mm(x, mod, row0, layer, g_pre, w, widx, bias=None):
    G, R, D = x.shape
    N = w.shape[2]
    tn = PRE_N_TILE
    gb, rb = _row_blocks(G, R, MM_ROWS)
    gidx = layer * N_SUB + 1
    has_bias = bias is not None
    body = functools.partial(_pre_mm_body, row0=row0, has_bias=has_bias)
    in_specs = [pl.BlockSpec((gb, rb, D), lambda g, t, n: (g, t, 0))] + _mod_specs(layer, 1, (0, 1), D) + [
        pl.BlockSpec((None, 1, D), lambda g, t, n: (gidx, 0, 0)),
        pl.BlockSpec((None, D, tn), lambda g, t, n: (widx, 0, n)),
    ]
    args = [x, mod, mod, g_pre, w]
    if has_bias:
        in_specs.append(pl.BlockSpec((None, 1, tn), lambda g, t, n: (widx, 0, n)))
        args.append(bias.reshape(bias.shape[0], 1, N))
    return pl.pallas_call(
        body,
        grid=(G // gb, R // rb, N // tn),
        in_specs=in_specs,
        out_specs=pl.BlockSpec((gb, rb, tn), lambda g, t, n: (g, t, n)),
        out_shape=jax.ShapeDtypeStruct((G, R, N), F32),
        scratch_shapes=[pltpu.VMEM((gb * rb, D), BF16)],
        compiler_params=_cparams(("arbitrary", "arbitrary", "arbitrary")),
        name="pre_mm",
    )(*args)


def _qkv_body(x_ref, sh_ref, sc_ref, gpre_ref, w_ref, qkv_ref, k_ref, v_ref, h_ref, *, row0, nb, q_scale):
    n = pl.program_id(2)
    gb, rb, _ = x_ref.shape
    seq0 = row0 + pl.program_id(0) * gb

    @pl.when(n == 0)
    def _():
        for b in range(gb):
            h = (_rms(x_ref[b], gpre_ref[...], NORM_EPS) * (1.0 + _mod_row(sc_ref, seq0, b))
                 + _mod_row(sh_ref, seq0, b))
            h_ref[b * rb:(b + 1) * rb, :] = h.astype(BF16)

    z = _dot(h_ref[...], w_ref[...])
    zs = z * jnp.where(n < nb, q_scale, 1.0)
    for b in range(gb):
        qkv_ref[b] = zs[b * rb:(b + 1) * rb, :].astype(BF16)

    @pl.when((n >= nb) & (n < 2 * nb))
    def _():
        for b in range(gb):
            k_ref[b] = z[b * rb:(b + 1) * rb, :]

    @pl.when(n >= 2 * nb)
    def _():
        for b in range(gb):
            v_ref[b] = z[b * rb:(b + 1) * rb, :]


def _qkv_proj(x, mod, row0, layer, g_pre, w, widx, q_scale):
    G, R, D = x.shape
    tn = PRE_N_TILE
    nb = D // tn
    gb, rb = _row_blocks(G, R, MM_ROWS)
    gidx = layer * N_SUB + 1
    body = functools.partial(_qkv_body, row0=row0, nb=nb, q_scale=q_scale)
    kspec = pl.BlockSpec((gb, rb, tn), lambda g, t, n: (g, t, jnp.clip(n - nb, 0, nb - 1)))
    vspec = pl.BlockSpec((gb, rb, tn), lambda g, t, n: (g, t, jnp.clip(n - 2 * nb, 0, nb - 1)))
    return pl.pallas_call(
        body,
        grid=(G // gb, R // rb, 3 * nb),
        in_specs=[pl.BlockSpec((gb, rb, D), lambda g, t, n: (g, t, 0))] + _mod_specs(layer, 1, (0, 1), D) + [
            pl.BlockSpec((None, 1, D), lambda g, t, n: (gidx, 0, 0)),
            pl.BlockSpec((None, D, tn), lambda g, t, n: (widx, 0, n))],
        out_specs=[pl.BlockSpec((gb, rb, tn), lambda g, t, n: (g, t, n)), kspec, vspec],
        out_shape=[jax.ShapeDtypeStruct((G, R, 3 * D), BF16), jax.ShapeDtypeStruct((G, R, D), F32),
                   jax.ShapeDtypeStruct((G, R, D), F32)],
        scratch_shapes=[pltpu.VMEM((gb * rb, D), BF16)],
        compiler_params=_cparams(("arbitrary", "arbitrary", "arbitrary")),
        name="qkv_proj",
    )(x, mod, mod, g_pre, w)


def _mm_post_body(y_ref, w_ref, x_ref, gt_ref, gpost_ref, *rest, row0, has_bias, has_ln):
    rest = list(rest)
    b_ref = rest.pop(0) if has_bias else None
    lng_ref, lnb_ref = (rest.pop(0), rest.pop(0)) if has_ln else (None, None)
    o_ref, a_ref = rest
    gb, rb, _ = x_ref.shape
    seq0 = row0 + pl.program_id(0) * gb

    for b in range(gb):
        y = y_ref[b]
        if has_ln:
            mu = jnp.mean(y, axis=-1, keepdims=True)
            yc = y - mu
            y = _silu(yc * lax.rsqrt(jnp.mean(yc * yc, axis=-1, keepdims=True) + LN_EPS) * lng_ref[...]
                      + lnb_ref[...])
        a_ref[b * rb:(b + 1) * rb, :] = y.astype(BF16)

    o = _dot(a_ref[...], w_ref[...])
    if has_bias:
        o = o + b_ref[...]
    for b in range(gb):
        o_ref[b] = x_ref[b] + _mod_row(gt_ref, seq0, b) * _rms(
            o[b * rb:(b + 1) * rb, :], gpost_ref[...], NORM_EPS)


def _mm_post(y, x, mod, row0, layer, g_post, w, widx, bias=None, ln=None):
    G, R, D = x.shape
    Din = y.shape[2]
    gb, rb = _row_blocks(G, R, MM_ROWS)
    gidx = layer * N_SUB + 1
    has_bias, has_ln = bias is not None, ln is not None
    body = functools.partial(_mm_post_body, row0=row0, has_bias=has_bias, has_ln=has_ln)
    xspec = pl.BlockSpec((gb, rb, D), lambda g, t: (g, t, 0))
    in_specs = [pl.BlockSpec((gb, rb, Din), lambda g, t: (g, t, 0)),
                pl.BlockSpec((None, Din, D), lambda g, t: (widx, 0, 0)),
                xspec] + _mod_specs(layer, 1, (2,), D) + [
        pl.BlockSpec((None, 1, D), lambda g, t: (gidx, 0, 0))]
    args = [y, w, x, mod, g_post]
    if has_bias:
        in_specs.append(pl.BlockSpec((None, 1, D), lambda g, t: (widx, 0, 0)))
        args.append(bias.reshape(bias.shape[0], 1, D))
    if has_ln:
        for p in ln:
            in_specs.append(pl.BlockSpec((None, 1, Din), lambda g, t: (widx, 0, 0)))
            args.append(p.reshape(p.shape[0], 1, Din))
    return pl.pallas_call(
        body,
        grid=(G // gb, R // rb),
        in_specs=in_specs,
        out_specs=xspec,
        out_shape=jax.ShapeDtypeStruct((G, R, D), F32),
        scratch_shapes=[pltpu.VMEM((gb * rb, Din), BF16)],
        compiler_params=_cparams(("arbitrary", "arbitrary")),
        name="mm_post",
    )(*args)


def _conv_core(ext_ref, w_ref, b, t, nt, hist_ref, nh_ref, new_rows, emit):
    W = w_ref.shape[0]
    rb = new_rows.shape[0]
    pad = ext_ref.shape[1] - rb
    lo = pad - (W - 1)

    @pl.when(t == 0)
    def _():
        ext_ref[b, lo:pad, :] = hist_ref[b]

    @pl.when(t > 0)
    def _():
        ext_ref[b, lo:pad, :] = ext_ref[b, lo + rb:pad + rb, :]

    ext_ref[b, pad:pad + rb, :] = new_rows
    rs = min(CONV_SUB_ROWS, rb)
    for r0 in range(0, rb, rs):
        acc = w_ref[0:1, :] * ext_ref[b, lo + r0:lo + r0 + rs, :]
        for j in range(1, W):
            acc = acc + w_ref[j:j + 1, :] * ext_ref[b, lo + j + r0:lo + j + r0 + rs, :]
        emit(r0, rs, acc)

    @pl.when(t == nt - 1)
    def _():
        nh_ref[b] = ext_ref[b, lo + rb:pad + rb, :]


def _conv_a_body(zb_ref, zc_ref, zx_ref, hist_ref, w_ref, y_ref, nh_ref, ext_ref, *, nt):
    t = pl.program_id(2)
    for b in range(zb_ref.shape[0]):
        def emit(r0, rs, acc, b=b):
            y_ref[b, r0:r0 + rs, :] = (zb_ref[b, r0:r0 + rs, :] * acc).astype(y_ref.dtype)
        _conv_core(ext_ref, w_ref, b, t, nt, hist_ref, nh_ref, zc_ref[b] * zx_ref[b], emit)


def _conv_b_body(za_ref, zg_ref, hist_ref, w_ref, bdw_ref, y_ref, nh_ref, ext_ref, *, nt):
    t = pl.program_id(2)
    for b in range(za_ref.shape[0]):
        def emit(r0, rs, acc, b=b):
            y_ref[b, r0:r0 + rs, :] = acc + bdw_ref[...]
        _conv_core(ext_ref, w_ref, b, t, nt, hist_ref, nh_ref,
                   za_ref[b] * jax.nn.sigmoid(zg_ref[b]), emit)


def _conv(kind, z, hist, w, widx, bdw=None):
    G, R, N = z.shape
    nsplit = 3 if kind == "a" else 2
    D = N // nsplit
    W = w.shape[1]
    gb, rb = _row_blocks(G, R, CONV_ROWS)
    assert rb >= W - 1
    tc = CONV_COLS
    nc = D // tc
    nt = R // rb
    pad = -(-(W - 1) // SUBLANES) * SUBLANES
    zspecs = [pl.BlockSpec((gb, rb, tc), lambda g, c, t, s=s: (g, t, s * nc + c)) for s in range(nsplit)]
    in_specs = zspecs + [pl.BlockSpec((gb, W - 1, tc), lambda g, c, t: (g, 0, c)),
                         pl.BlockSpec((None, W, tc), lambda g, c, t: (widx, 0, c))]
    args = [z] * nsplit + [hist, w]
    if kind == "a":
        body, out_dtype = _conv_a_body, BF16
    else:
        body, out_dtype = _conv_b_body, F32
        in_specs.append(pl.BlockSpec((None, 1, tc), lambda g, c, t: (widx, 0, c)))
        args.append(bdw.reshape(bdw.shape[0], 1, D))
    return pl.pallas_call(
        functools.partial(body, nt=nt),
        grid=(G // gb, nc, nt),
        in_specs=in_specs,
        out_specs=[pl.BlockSpec((gb, rb, tc), lambda g, c, t: (g, t, c)),
                   pl.BlockSpec((gb, W - 1, tc), lambda g, c, t: (g, 0, c))],
        out_shape=[jax.ShapeDtypeStruct((G, R, D), out_dtype),
                   jax.ShapeDtypeStruct((G, W - 1, D), F32)],
        scratch_shapes=[pltpu.VMEM((gb, pad + rb, tc), F32)],
        compiler_params=_cparams(("arbitrary", "arbitrary", "arbitrary")),
        name="conv_" + kind,
    )(*args)


def _last_visible_tile(q_pos, tk):
    return ((q_pos // CHUNK) * CHUNK + CHUNK - 1) // tk


def _attn_body(slope_ref, lam_ref, subg_ref, q_ref, k_ref, v_ref, o_ref, m_ref, l_ref, acc_ref, ramp_ref,
               *, pos0, n_keys, nk, lam_init):
    h, qi, ki = pl.program_id(1), pl.program_id(2), pl.program_id(3)
    tq, dh2 = q_ref.shape
    tk = k_ref.shape[0]
    dh = dh2 // 2
    shift = int(math.log2(CHUNK))
    slope = slope_ref[h]

    @pl.when((qi == 0) & (ki == 0))
    def _():
        d = lax.broadcasted_iota(jnp.int32, (tk, tq), 1) - lax.broadcasted_iota(jnp.int32, (tk, tq), 0)
        ramp_ref[...] = -slope * d.astype(F32)

    @pl.when(ki == 0)
    def _():
        m_ref[...] = jnp.full(m_ref.shape, NEG_INF, F32)
        l_ref[...] = jnp.zeros(l_ref.shape, F32)
        acc_ref[...] = jnp.zeros(acc_ref.shape, F32)

    q_lo = pos0 + qi * tq
    k_lo = ki * tk

    def raw_scores(c):
        return lax.dot_general(k_ref[:, c * dh:(c + 1) * dh], q_ref[:, c * dh:(c + 1) * dh],
                               (((1,), (1,)), ((), ())), preferred_element_type=F32)

    def online_update(c, t, cst):
        m_old = m_ref[c]
        m_new = jnp.maximum(m_old, jnp.max(t, axis=0, keepdims=True) + cst)
        alpha = jnp.exp2(m_old - m_new)
        p = jnp.exp2(t + (cst - m_new))
        l_ref[c] = alpha * l_ref[c] + jnp.sum(p, axis=0, keepdims=True)
        pv = lax.dot_general(v_ref[...], p.astype(BF16), (((0,), (0,)), ((), ())), preferred_element_type=F32)
        acc_ref[c] = alpha * acc_ref[c] + pv
        m_ref[c] = m_new

    all_past = k_lo + tk - 1 < q_lo

    @pl.when(all_past)
    def _():
        cst = -slope * (q_lo - k_lo).astype(F32)
        for c in range(2):
            online_update(c, raw_scores(c) + ramp_ref[...], cst)

    @pl.when(jnp.logical_not(all_past) & (ki <= _last_visible_tile(q_lo + tq - 1, tk)))
    def _():
        qpos = q_lo + lax.broadcasted_iota(jnp.int32, (1, tq), 1)
        kpos = k_lo + lax.broadcasted_iota(jnp.int32, (tk, 1), 0)
        visible = (jnp.right_shift(kpos, shift) <= jnp.right_shift(qpos, shift)) & (kpos < n_keys)
        bias = -slope * jnp.abs(qpos - kpos).astype(F32)
        for c in range(2):
            online_update(c, jnp.where(visible, raw_scores(c) + bias, NEG_INF), 0.0)

    @pl.when(ki == nk - 1)
    def _():
        lp = lam_ref[...]
        lam = (jnp.exp(jnp.sum(lp[0:1] * lp[1:2], axis=-1, keepdims=True))
               - jnp.exp(jnp.sum(lp[2:3] * lp[3:4], axis=-1, keepdims=True)) + lam_init)
        o = acc_ref[0] / l_ref[0] - lam * (acc_ref[1] / l_ref[1])
        r = o * lax.rsqrt(jnp.mean(o * o, axis=0, keepdims=True) + SUBLN_EPS) * subg_ref[...]
        o_ref[...] = (r * (1.0 - lam_init)).T.astype(o_ref.dtype)


def _diff_attn(q_arr, q_off, k_arr, k_off, v_arr, v_off, n_keys, pos0, lam_params, subln_g, idx, lam_init,
               slopes):
    slopes_log2 = slopes * LOG2E
    B, Tq, _ = q_arr.shape
    Tk = k_arr.shape[1]
    dh2 = subln_g.shape[1]
    H = N_HEADS
    tq = min(ATTN_Q_TILE, Tq)
    tk = Tk if Tk <= 2 * ATTN_K_TILE + LANES else ATTN_K_TILE
    assert Tq % tq == 0 and Tk % tk == 0 and pos0 + Tq <= n_keys <= Tk
    nq, nk = Tq // tq, Tk // tk
    assert 1 << int(math.log2(CHUNK)) == CHUNK

    def k_index(off):
        def index(b, h, qi, ki):
            last = jnp.minimum(_last_visible_tile(pos0 + (qi + 1) * tq - 1, tk), nk - 1)
            return (b, jnp.minimum(ki, last), off + h)
        return index

    body = functools.partial(_attn_body, pos0=pos0, n_keys=n_keys, nk=nk, lam_init=lam_init)
    return pl.pallas_call(
        body,
        grid=(B, H, nq, nk),
        in_specs=[pl.BlockSpec(memory_space=pltpu.SMEM),
                  pl.BlockSpec((None, 4, dh2 // 2), lambda b, h, qi, ki: (idx, 0, 0)),
                  pl.BlockSpec((None, dh2, 1), lambda b, h, qi, ki: (idx, 0, 0)),
                  pl.BlockSpec((None, tq, dh2), lambda b, h, qi, ki: (b, qi, q_off + h)),
                  pl.BlockSpec((None, tk, dh2), k_index(k_off)),
                  pl.BlockSpec((None, tk, dh2), k_index(v_off))],
        out_specs=pl.BlockSpec((None, tq, dh2), lambda b, h, qi, ki: (b, qi, h)),
        out_shape=jax.ShapeDtypeStruct((B, Tq, H * dh2), BF16),
        scratch_shapes=[pltpu.VMEM((2, 1, tq), F32), pltpu.VMEM((2, 1, tq), F32),
                        pltpu.VMEM((2, dh2, tq), F32), pltpu.VMEM((tk, tq), F32)],
        compiler_params=_cparams(("arbitrary", "arbitrary", "arbitrary", "arbitrary")),
        name="diff_attn",
    )(slopes_log2, lam_params, subln_g.reshape(subln_g.shape[0], dh2, 1), q_arr, k_arr, v_arr)


def _trunk(x, row0, mod, hist_a, hist_b, hist_k, hist_v, pos0, p):
    G, R, D = x.shape
    depth = p["ffn_w_gu"].shape[0]
    slopes = jnp.asarray([2.0 ** (-8.0 * (h + 1) / N_HEADS) for h in range(N_HEADS)], dtype=F32)
    new_a, new_b, new_k, new_v = [], [], [], []
    for i in range(depth):
        x = _ffn(x, mod, row0, i, 0, 0, p["g_pre"], p["g_post"], p["ffn_w_gu"], p["ffn_w_down"])
        kind, idx = i % N_MIXERS, i // N_MIXERS
        if kind == 0:
            z = _pre_mm(x, mod, row0, i, p["g_pre"], p["sc_w_in"], idx)
            y, st = _conv("a", z, hist_a[idx], p["sc_w_conv"], idx)
            new_a.append(st)
            x = _mm_post(y, x, mod, row0, i, p["g_post"], p["sc_w_out"], idx)
        elif kind == 1:
            z = _pre_mm(x, mod, row0, i, p["g_pre"], p["cm_w_pw1"], idx, bias=p["cm_b_pw1"])
            y, st = _conv("b", z, hist_b[idx], p["cm_w_dw"], idx, bdw=p["cm_b_dw"])
            new_b.append(st)
            x = _mm_post(y, x, mod, row0, i, p["g_post"], p["cm_w_pw2"], idx, bias=p["cm_b_pw2"],
                         ln=(p["cm_ln_g"], p["cm_ln_b"]))
        else:
            lam_init = 0.8 - 0.6 * math.exp(-0.3 * i)
            dh2 = p["attn_subln_g"].shape[1]
            qkv, k_rows, v_rows = _qkv_proj(x, mod, row0, i, p["g_pre"], p["attn_w_qkv"], idx,
                                            (dh2 // 2) ** -0.5 * LOG2E)
            new_k.append(k_rows)
            new_v.append(v_rows)
            nblk = D // dh2
            if hist_k is None:
                o = _diff_attn(qkv, 0, qkv, nblk, qkv, 2 * nblk, R, pos0, p["attn_lambda"], p["attn_subln_g"],
                               idx, lam_init, slopes)
            else:
                past = hist_k.shape[2]
                n_keys = past + R
                padn = -n_keys % LANES
                zpad = jnp.zeros((G, padn, D), BF16)
                k_all = jnp.concatenate([hist_k[idx].reshape(G, past, D).astype(BF16), qkv[:, :, D:2 * D], zpad],
                                        axis=1)
                v_all = jnp.concatenate([hist_v[idx].reshape(G, past, D).astype(BF16), qkv[:, :, 2 * D:], zpad],
                                        axis=1)
                o = _diff_attn(qkv, 0, k_all, 0, v_all, 0, n_keys, pos0, p["attn_lambda"], p["attn_subln_g"],
                               idx, lam_init, slopes)
            x = _mm_post(o, x, mod, row0, i, p["g_post"], p["attn_w_o"], idx)
        x = _ffn(x, mod, row0, i, 2, 1, p["g_pre"], p["g_post"], p["ffn_w_gu"], p["ffn_w_down"])
    return x, jnp.stack(new_a), jnp.stack(new_b), jnp.stack(new_k), jnp.stack(new_v)


def kernel(x_prompt, x_sample, state_conv_a, state_conv_b, cache_k, cache_v, c_prompt, c_sample, w_mod, b_mod, g_pre, g_post, ffn_w_gu, ffn_w_down, sc_w_in, sc_w_conv, sc_w_out, cm_w_pw1, cm_b_pw1, cm_w_dw, cm_b_dw, cm_ln_g, cm_ln_b, cm_w_pw2, cm_b_pw2, attn_w_qkv, attn_lambda, attn_subln_g, attn_w_o):
    Bp, T, D = x_prompt.shape
    Bs = x_sample.shape[0]
    depth = g_pre.shape[0]
    n_a, n_b = state_conv_a.shape[0], state_conv_b.shape[0]
    assert Bp + Bs <= SEQ_PAD
    H, dh2 = N_HEADS, attn_subln_g.shape[1]

    c_all = jnp.concatenate([c_prompt, c_sample, jnp.zeros((SEQ_PAD - Bp - Bs, D), F32)], axis=0)
    mod = _adaln_mod(c_all, w_mod, b_mod)

    p = dict(
        g_pre=g_pre.reshape(depth * N_SUB, 1, D), g_post=g_post.reshape(depth * N_SUB, 1, D),
        ffn_w_gu=ffn_w_gu.astype(BF16), ffn_w_down=ffn_w_down.astype(BF16),
        sc_w_in=sc_w_in.astype(BF16), sc_w_conv=sc_w_conv, sc_w_out=sc_w_out.astype(BF16),
        cm_w_pw1=cm_w_pw1.astype(BF16), cm_b_pw1=cm_b_pw1, cm_w_dw=cm_w_dw, cm_b_dw=cm_b_dw,
        cm_ln_g=cm_ln_g, cm_ln_b=cm_ln_b, cm_w_pw2=cm_w_pw2.astype(BF16), cm_b_pw2=cm_b_pw2,
        attn_w_qkv=attn_w_qkv.astype(BF16), attn_lambda=attn_lambda, attn_subln_g=attn_subln_g,
        attn_w_o=attn_w_o.astype(BF16))

    zeros_a = jnp.zeros((n_a, Bp) + state_conv_a.shape[2:], F32)
    zeros_b = jnp.zeros((n_b, Bp) + state_conv_b.shape[2:], F32)
    y_p, a_p, b_p, k_p, v_p = _trunk(x_prompt, 0, mod, zeros_a, zeros_b, None, None, 0, p)
    past = cache_k.shape[2]
    y_s, a_s, b_s, k_s, v_s = _trunk(x_sample, Bp, mod, state_conv_a, state_conv_b, cache_k, cache_v, past, p)

    def heads(a):
        return a.reshape(a.shape[:3] + (H, dh2))
    return (y_p, y_s, a_p, a_s, b_p, b_s, heads(k_p), heads(v_p), heads(k_s), heads(v_s))
```

```python
import functools
import math

import numpy as np
import jax
import jax.numpy as jnp
from jax import lax
from jax.experimental import pallas as pl
from jax.experimental.pallas import tpu as pltpu

F32 = jnp.float32
BF16 = jnp.bfloat16

N_HEADS = 8
CHUNK = 64
FFN_RES = 0.5
NORM_EPS = 1e-6
SUBLN_EPS = 1e-5
LN_EPS = 1e-5
NEG_INF = -1e30
LOG2E = math.log2(math.e)
N_SUB = 3
N_MIXERS = 3

SEQ_PAD = 16
V7X_VMEM_LIMIT_BYTES = 56 * 1024 * 1024
SUBLANES = 8
LANES = 128
ROW_BLK = 16
ROW_UNROLL = 4

FFN_ROWS = 1024
FFN_FF_TILE = 512
FFN_DOWN_TILE = 512
MM_ROWS = 512
PRE_N_TILE = 1024
CONV_ROWS = 512
CONV_COLS = 256
CONV_SUB_ROWS = 64
ATTN_Q_TILE = 512
ATTN_K_TILE = 512


def _cparams(sem):
    return pltpu.CompilerParams(dimension_semantics=sem, vmem_limit_bytes=V7X_VMEM_LIMIT_BYTES)


def _row_blocks(G, R, rows):
    if R >= rows:
        assert R % rows == 0
        return 1, rows
    gb = min(G, rows // R)
    assert G % gb == 0 and R % ROW_BLK == 0
    return gb, R


def _silu(x):
    return x * jax.nn.sigmoid(x)


def _dot(a, b):
    return jnp.dot(a, b, preferred_element_type=F32)


def _mod_row(ref, seq0, b):
    return ref[pl.ds(seq0 + b, 1), :]


def _row_loop(n_rows, fn, static=False):
    nblk = n_rows // ROW_BLK
    if static:
        for i in range(nblk):
            fn(i * ROW_BLK)
        return

    def step(i, carry):
        fn(pl.multiple_of(i * ROW_BLK, ROW_BLK))
        return carry
    lax.fori_loop(0, nblk, step, 0, unroll=min(ROW_UNROLL, nblk))


def _set_vec(vec_ref, k, row):
    vec_ref[k] = jnp.broadcast_to(row, vec_ref.shape[1:])


def _prenorm_rows(x_ref, h_ref, vec_ref, gpre_ref, sh_ref, sc_ref, seq0, zero_ref=None):
    gb, rb, _ = x_ref.shape
    for b in range(gb):
        _set_vec(vec_ref, 0, gpre_ref[...] * (1.0 + _mod_row(sc_ref, seq0, b)))
        _set_vec(vec_ref, 1, _mod_row(sh_ref, seq0, b))

        def blk(r0, b=b):
            x = x_ref[b, r0:r0 + ROW_BLK, :]
            r = lax.rsqrt(jnp.mean(x * x, axis=-1, keepdims=True) + NORM_EPS)
            h_ref[b * rb + r0:b * rb + r0 + ROW_BLK, :] = ((x * r) * vec_ref[0] + vec_ref[1]).astype(BF16)
            if zero_ref is not None:
                zero_ref[b, r0:r0 + ROW_BLK, :] = jnp.zeros((ROW_BLK, zero_ref.shape[2]), F32)
        _row_loop(rb, blk, static=True)


def _postnorm_rows(o_ref, x_ref, vec_ref, gpost_ref, gt_ref, seq0, rw):
    gb, rb, _ = x_ref.shape
    for b in range(gb):
        _set_vec(vec_ref, 0, (rw * _mod_row(gt_ref, seq0, b)) * gpost_ref[...])

        def blk(r0, b=b):
            o = o_ref[b, r0:r0 + ROW_BLK, :]
            r = lax.rsqrt(jnp.mean(o * o, axis=-1, keepdims=True) + NORM_EPS)
            o_ref[b, r0:r0 + ROW_BLK, :] = x_ref[b, r0:r0 + ROW_BLK, :] + (o * r) * vec_ref[0]
        _row_loop(rb, blk, static=True)


def _adaln_body(c_ref, w_ref, b_ref, o_ref):
    a = _silu(c_ref[...]).astype(BF16)
    o_ref[...] = _dot(a, w_ref[...].astype(BF16)) + b_ref[...]


def _adaln_mod(c_all, w_mod, b_mod):
    L, D, N = w_mod.shape
    nm = N // D
    P = c_all.shape[0]
    return pl.pallas_call(
        _adaln_body,
        grid=(L, nm),
        in_specs=[pl.BlockSpec((P, D), lambda l, m: (0, 0)),
                  pl.BlockSpec((None, D, D), lambda l, m: (l, 0, m)),
                  pl.BlockSpec((None, 1, D), lambda l, m: (l * nm + m, 0, 0))],
        out_specs=pl.BlockSpec((None, P, D), lambda l, m: (l * nm + m, 0, 0)),
        out_shape=jax.ShapeDtypeStruct((L * nm, P, D), F32),
        compiler_params=_cparams(("arbitrary", "arbitrary")),
        name="adaln_mod",
    )(c_all, w_mod, b_mod.reshape(L * nm, 1, D))


def _mod_specs(layer, sub, kinds, D):
    specs = []
    for k in kinds:
        idx = layer * 3 * N_SUB + 3 * sub + k
        specs.append(pl.BlockSpec((None, SEQ_PAD, D), lambda *_, idx=idx: (idx, 0, 0)))
    return specs


def _ffn_body(x_ref, sh_ref, sc_ref, gt_ref, gpre_ref, gpost_ref, wg_ref, wu_ref, wd_ref, *rest,
              row0, n_chunks, emit):
    if emit:
        o_ref, wgo_ref, wuo_ref, wdo_ref, h_ref, vec_ref = rest
    else:
        o_ref, h_ref, vec_ref = rest
    j = pl.program_id(2)
    gb, rb, _ = x_ref.shape
    seq0 = row0 + pl.program_id(0) * gb

    @pl.when(j == 0)
    def _():
        _prenorm_rows(x_ref, h_ref, vec_ref, gpre_ref, sh_ref, sc_ref, seq0, zero_ref=o_ref)

    if emit:
        wg, wu, wd = wg_ref[...].astype(BF16), wu_ref[...].astype(BF16), wd_ref[...].astype(BF16)
        wgo_ref[...] = wg
        wuo_ref[...] = wu
        wdo_ref[...] = wd
    else:
        wg, wu, wd = wg_ref[...], wu_ref[...], wd_ref

    h = h_ref[...]
    act = (_silu(_dot(h, wg)) * _dot(h, wu)).astype(BF16)
    dn = min(FFN_DOWN_TILE, o_ref.shape[2])
    for n0 in range(0, o_ref.shape[2], dn):
        y = _dot(act, wd[:, n0:n0 + dn])
        for b in range(gb):
            o_ref[b, :, n0:n0 + dn] += y[b * rb:(b + 1) * rb, :]

    @pl.when(j == n_chunks - 1)
    def _():
        _postnorm_rows(o_ref, x_ref, vec_ref, gpost_ref, gt_ref, seq0, FFN_RES)


def _ffn(x, mod, row0, layer, sub, g_pre, g_post, weights):
    G, R, D = x.shape
    emit = weights[0] == "f32"
    tf = FFN_FF_TILE
    gb, rb = _row_blocks(G, R, FFN_ROWS)
    grid_rows = (G // gb, R // rb)
    if emit:
        _, w_gu, w_down, ffn_idx = weights
        dff = w_down.shape[2]
        n_chunks = dff // tf
        assert grid_rows == (1, 1)
        wspecs = [pl.BlockSpec((None, None, D, tf), lambda g, t, j: (layer, ffn_idx, 0, j)),
                  pl.BlockSpec((None, None, D, tf), lambda g, t, j: (layer, ffn_idx, 0, j + n_chunks)),
                  pl.BlockSpec((None, None, tf, D), lambda g, t, j: (layer, ffn_idx, j, 0))]
        wargs = [w_gu, w_gu, w_down]
    else:
        _, wg, wu, wd = weights
        dff = wd.shape[0]
        n_chunks = dff // tf
        wargs = [wg, wu, wd]
    copy_specs = [pl.BlockSpec((D, tf), lambda g, t, j: (0, j)),
                  pl.BlockSpec((D, tf), lambda g, t, j: (0, j)),
                  pl.BlockSpec((tf, D), lambda g, t, j: (j, 0))]
    if not emit:
        wspecs = copy_specs
    gidx = layer * N_SUB + sub
    body = functools.partial(_ffn_body, row0=row0, n_chunks=n_chunks, emit=emit)
    xspec = pl.BlockSpec((gb, rb, D), lambda g, t, j: (g, t, 0))
    xin_spec = pl.BlockSpec((gb, rb, D), lambda g, t, j: (g, t, 0), pipeline_mode=pl.Buffered(1))
    out_specs, out_shape = [xspec], [jax.ShapeDtypeStruct((G, R, D), F32)]
    if emit:
        out_specs += copy_specs
        out_shape += [jax.ShapeDtypeStruct((D, dff), BF16), jax.ShapeDtypeStruct((D, dff), BF16),
                      jax.ShapeDtypeStruct((dff, D), BF16)]
    outs = pl.pallas_call(
        body,
        grid=grid_rows + (n_chunks,),
        in_specs=[xin_spec] + _mod_specs(layer, sub, (0, 1, 2), D) + [
            pl.BlockSpec((None, 1, D), lambda g, t, j: (gidx, 0, 0)),
            pl.BlockSpec((None, 1, D), lambda g, t, j: (gidx, 0, 0))] + wspecs,
        out_specs=out_specs,
        out_shape=out_shape,
        scratch_shapes=[pltpu.VMEM((gb * rb, D), BF16), pltpu.VMEM((2, ROW_BLK, D), F32)],
        compiler_params=_cparams(("arbitrary", "arbitrary", "arbitrary")),
        name="ffn_cast" if emit else "ffn",
    )(x, mod, mod, mod, g_pre, g_post, *wargs)
    return (outs[0], tuple(outs[1:])) if emit else (outs[0], None)


def _pre_mm_body(x_ref, sh_ref, sc_ref, gpre_ref, w_ref, *rest, row0, has_bias):
    if has_bias:
        b_ref, o_ref, h_ref, vec_ref = rest
    else:
        o_ref, h_ref, vec_ref = rest
    gb, rb, _ = x_ref.shape
    seq0 = row0 + pl.program_id(0) * gb

    @pl.when(pl.program_id(2) == 0)
    def _():
        _prenorm_rows(x_ref, h_ref, vec_ref, gpre_ref, sh_ref, sc_ref, seq0)

    z = _dot(h_ref[...], w_ref[...])
    if has_bias:
        z = z + b_ref[...]
    for b in range(gb):
        o_ref[b] = z[b * rb:(b + 1) * rb, :]


def _pre_mm(x, mod, row0, layer, g_pre, w, widx, bias=None):
    G, R, D = x.shape
    N = w.shape[2]
    tn = PRE_N_TILE
    gb, rb = _row_blocks(G, R, MM_ROWS)
    gidx = layer * N_SUB + 1
    has_bias = bias is not None
    body = functools.partial(_pre_mm_body, row0=row0, has_bias=has_bias)
    in_specs = [pl.BlockSpec((gb, rb, D), lambda g, t, n: (g, t, 0))] + _mod_specs(layer, 1, (0, 1), D) + [
        pl.BlockSpec((None, 1, D), lambda g, t, n: (gidx, 0, 0)),
        pl.BlockSpec((None, D, tn), lambda g, t, n: (widx, 0, n)),
    ]
    args = [x, mod, mod, g_pre, w]
    if has_bias:
        in_specs.append(pl.BlockSpec((None, 1, tn), lambda g, t, n: (widx, 0, n)))
        args.append(bias.reshape(bias.shape[0], 1, N))
    return pl.pallas_call(
        body,
        grid=(G // gb, R // rb, N // tn),
        in_specs=in_specs,
        out_specs=pl.BlockSpec((gb, rb, tn), lambda g, t, n: (g, t, n)),
        out_shape=jax.ShapeDtypeStruct((G, R, N), F32),
        scratch_shapes=[pltpu.VMEM((gb * rb, D), BF16), pltpu.VMEM((2, ROW_BLK, D), F32)],
        compiler_params=_cparams(("arbitrary", "arbitrary", "arbitrary")),
        name="pre_mm",
    )(*args)


def _qkv_body(x_ref, sh_ref, sc_ref, gpre_ref, w_ref, qkv_ref, k_ref, v_ref, h_ref, vec_ref, *, row0, nb, q_scale):
    n = pl.program_id(2)
    gb, rb, _ = x_ref.shape
    seq0 = row0 + pl.program_id(0) * gb

    @pl.when(n == 0)
    def _():
        _prenorm_rows(x_ref, h_ref, vec_ref, gpre_ref, sh_ref, sc_ref, seq0)

    z = _dot(h_ref[...], w_ref[...])
    zs = z * jnp.where(n < nb, q_scale, 1.0)
    for b in range(gb):
        qkv_ref[b] = zs[b * rb:(b + 1) * rb, :].astype(BF16)

    @pl.when((n >= nb) & (n < 2 * nb))
    def _():
        for b in range(gb):
            k_ref[b] = z[b * rb:(b + 1) * rb, :]

    @pl.when(n >= 2 * nb)
    def _():
        for b in range(gb):
            v_ref[b] = z[b * rb:(b + 1) * rb, :]


def _qkv_proj(x, mod, row0, layer, g_pre, w, widx, q_scale):
    G, R, D = x.shape
    tn = PRE_N_TILE
    nb = D // tn
    gb, rb = _row_blocks(G, R, MM_ROWS)
    gidx = layer * N_SUB + 1
    body = functools.partial(_qkv_body, row0=row0, nb=nb, q_scale=q_scale)
    kspec = pl.BlockSpec((gb, rb, tn), lambda g, t, n: (g, t, jnp.clip(n - nb, 0, nb - 1)))
    vspec = pl.BlockSpec((gb, rb, tn), lambda g, t, n: (g, t, jnp.clip(n - 2 * nb, 0, nb - 1)))
    return pl.pallas_call(
        body,
        grid=(G // gb, R // rb, 3 * nb),
        in_specs=[pl.BlockSpec((gb, rb, D), lambda g, t, n: (g, t, 0))] + _mod_specs(layer, 1, (0, 1), D) + [
            pl.BlockSpec((None, 1, D), lambda g, t, n: (gidx, 0, 0)),
            pl.BlockSpec((None, D, tn), lambda g, t, n: (widx, 0, n))],
        out_specs=[pl.BlockSpec((gb, rb, tn), lambda g, t, n: (g, t, n)), kspec, vspec],
        out_shape=[jax.ShapeDtypeStruct((G, R, 3 * D), BF16), jax.ShapeDtypeStruct((G, R, D), F32),
                   jax.ShapeDtypeStruct((G, R, D), F32)],
        scratch_shapes=[pltpu.VMEM((gb * rb, D), BF16), pltpu.VMEM((2, ROW_BLK, D), F32)],
        compiler_params=_cparams(("arbitrary", "arbitrary", "arbitrary")),
        name="qkv_proj",
    )(x, mod, mod, g_pre, w)


def _mm_post_body(y_ref, w_ref, x_ref, gt_ref, gpost_ref, *rest, row0, has_bias, has_ln):
    rest = list(rest)
    b_ref = rest.pop(0) if has_bias else None
    lng_ref, lnb_ref = (rest.pop(0), rest.pop(0)) if has_ln else (None, None)
    o_ref = rest.pop(0)
    a_ref = rest.pop(0) if has_ln else None
    vec_ref = rest.pop(0)
    gb, rb, _ = x_ref.shape
    seq0 = row0 + pl.program_id(0) * gb

    if has_ln:
        for b in range(gb):
            def blk(r0, b=b):
                y = y_ref[b, pl.ds(r0, ROW_BLK), :]
                yc = y - jnp.mean(y, axis=-1, keepdims=True)
                n = yc * lax.rsqrt(jnp.mean(yc * yc, axis=-1, keepdims=True) + LN_EPS) * lng_ref[...] + lnb_ref[...]
                a_ref[pl.ds(b * rb + r0, ROW_BLK), :] = _silu(n).astype(BF16)
            _row_loop(rb, blk)
        a = a_ref[...]
    else:
        a = y_ref[...].reshape(gb * rb, y_ref.shape[2])

    o = _dot(a, w_ref[...])
    if has_bias:
        o = o + b_ref[...]
    for b in range(gb):
        o_ref[b] = o[b * rb:(b + 1) * rb, :]
    _postnorm_rows(o_ref, x_ref, vec_ref, gpost_ref, gt_ref, seq0, 1.0)


def _mm_post(y, x, mod, row0, layer, g_post, w, widx, bias=None, ln=None):
    G, R, D = x.shape
    Din = y.shape[2]
    gb, rb = _row_blocks(G, R, MM_ROWS)
    gidx = layer * N_SUB + 1
    has_bias, has_ln = bias is not None, ln is not None
    assert has_ln or y.dtype == BF16
    body = functools.partial(_mm_post_body, row0=row0, has_bias=has_bias, has_ln=has_ln)
    xspec = pl.BlockSpec((gb, rb, D), lambda g, t: (g, t, 0))
    in_specs = [pl.BlockSpec((gb, rb, Din), lambda g, t: (g, t, 0)),
                pl.BlockSpec((None, Din, D), lambda g, t: (widx, 0, 0)),
                xspec] + _mod_specs(layer, 1, (2,), D) + [
        pl.BlockSpec((None, 1, D), lambda g, t: (gidx, 0, 0))]
    args = [y, w, x, mod, g_post]
    if has_bias:
        in_specs.append(pl.BlockSpec((None, 1, D), lambda g, t: (widx, 0, 0)))
        args.append(bias.reshape(bias.shape[0], 1, D))
    if has_ln:
        for p in ln:
            in_specs.append(pl.BlockSpec((None, 1, Din), lambda g, t: (widx, 0, 0)))
            args.append(p.reshape(p.shape[0], 1, Din))
    scratch = ([pltpu.VMEM((gb * rb, Din), BF16)] if has_ln else []) + [pltpu.VMEM((2, ROW_BLK, D), F32)]
    return pl.pallas_call(
        body,
        grid=(G // gb, R // rb),
        in_specs=in_specs,
        out_specs=xspec,
        out_shape=jax.ShapeDtypeStruct((G, R, D), F32),
        scratch_shapes=scratch,
        compiler_params=_cparams(("arbitrary", "arbitrary")),
        name="mm_post",
    )(*args)


def _conv_core(ext_ref, w_ref, b, t, nt, hist_ref, nh_ref, new_rows, emit):
    W = w_ref.shape[0]
    rb = new_rows.shape[0]
    pad = ext_ref.shape[1] - rb
    lo = pad - (W - 1)

    @pl.when(t == 0)
    def _():
        ext_ref[b, lo:pad, :] = hist_ref[b]

    @pl.when(t > 0)
    def _():
        ext_ref[b, lo:pad, :] = ext_ref[b, lo + rb:pad + rb, :]

    ext_ref[b, pad:pad + rb, :] = new_rows
    rs = min(CONV_SUB_ROWS, rb)
    for r0 in range(0, rb, rs):
        acc = w_ref[0:1, :] * ext_ref[b, lo + r0:lo + r0 + rs, :]
        for j in range(1, W):
            acc = acc + w_ref[j:j + 1, :] * ext_ref[b, lo + j + r0:lo + j + r0 + rs, :]
        emit(r0, rs, acc)

    @pl.when(t == nt - 1)
    def _():
        nh_ref[b] = ext_ref[b, lo + rb:pad + rb, :]


def _conv_a_body(zb_ref, zc_ref, zx_ref, hist_ref, w_ref, y_ref, nh_ref, ext_ref, *, nt):
    t = pl.program_id(2)
    for b in range(zb_ref.shape[0]):
        def emit(r0, rs, acc, b=b):
            y_ref[b, r0:r0 + rs, :] = (zb_ref[b, r0:r0 + rs, :] * acc).astype(y_ref.dtype)
        _conv_core(ext_ref, w_ref, b, t, nt, hist_ref, nh_ref, zc_ref[b] * zx_ref[b], emit)


def _conv_b_body(za_ref, zg_ref, hist_ref, w_ref, bdw_ref, y_ref, nh_ref, ext_ref, *, nt):
    t = pl.program_id(2)
    for b in range(za_ref.shape[0]):
        def emit(r0, rs, acc, b=b):
            y_ref[b, r0:r0 + rs, :] = acc + bdw_ref[...]
        _conv_core(ext_ref, w_ref, b, t, nt, hist_ref, nh_ref,
                   za_ref[b] * jax.nn.sigmoid(zg_ref[b]), emit)


def _conv(kind, z, hist, w, widx, bdw=None):
    G, R, N = z.shape
    nsplit = 3 if kind == "a" else 2
    D = N // nsplit
    W = w.shape[1]
    gb, rb = _row_blocks(G, R, CONV_ROWS)
    assert rb >= W - 1
    tc = CONV_COLS
    nc = D // tc
    nt = R // rb
    pad = -(-(W - 1) // SUBLANES) * SUBLANES
    zspecs = [pl.BlockSpec((gb, rb, tc), lambda g, c, t, s=s: (g, t, s * nc + c)) for s in range(nsplit)]
    in_specs = zspecs + [pl.BlockSpec((gb, W - 1, tc), lambda g, c, t: (g, 0, c)),
                         pl.BlockSpec((None, W, tc), lambda g, c, t: (widx, 0, c))]
    args = [z] * nsplit + [hist, w]
    if kind == "a":
        body, out_dtype = _conv_a_body, BF16
    else:
        body, out_dtype = _conv_b_body, F32
        in_specs.append(pl.BlockSpec((None, 1, tc), lambda g, c, t: (widx, 0, c)))
        args.append(bdw.reshape(bdw.shape[0], 1, D))
    return pl.pallas_call(
        functools.partial(body, nt=nt),
        grid=(G // gb, nc, nt),
        in_specs=in_specs,
        out_specs=[pl.BlockSpec((gb, rb, tc), lambda g, c, t: (g, t, c)),
                   pl.BlockSpec((gb, W - 1, tc), lambda g, c, t: (g, 0, c))],
        out_shape=[jax.ShapeDtypeStruct((G, R, D), out_dtype),
                   jax.ShapeDtypeStruct((G, W - 1, D), F32)],
        scratch_shapes=[pltpu.VMEM((gb, pad + rb, tc), F32)],
        compiler_params=_cparams(("arbitrary", "arbitrary", "arbitrary")),
        name="conv_" + kind,
    )(*args)


def _last_visible_tile(q_pos, tk):
    return ((q_pos // CHUNK) * CHUNK + CHUNK - 1) // tk


def _lambda(lam_ref, lam_init):
    lp = lam_ref[...]
    return (jnp.exp(jnp.sum(lp[0:1] * lp[1:2], axis=-1, keepdims=True))
            - jnp.exp(jnp.sum(lp[2:3] * lp[3:4], axis=-1, keepdims=True)) + lam_init)


def _diff_out(o1, o2, lam, subg, lam_init):
    o = o1 - lam * o2
    r = o * lax.rsqrt(jnp.mean(o * o, axis=0, keepdims=True) + SUBLN_EPS) * subg
    return (r * (1.0 - lam_init)).T


def _attn_body(qi_ref, ki_ref, last_ref, slope_ref, lam_ref, subg_ref, q_ref, k_ref, v_ref, o_ref,
               m_ref, l_ref, acc_ref, ramp_ref, *, pos0, n_keys, lam_init):
    h, s = pl.program_id(1), pl.program_id(2)
    qi, ki = qi_ref[s], ki_ref[s]
    tq, dh2 = q_ref.shape
    tk = k_ref.shape[0]
    dh = dh2 // 2
    shift = int(math.log2(CHUNK))
    slope = slope_ref[h]

    @pl.when(s == 0)
    def _():
        d = lax.broadcasted_iota(jnp.int32, (tk, tq), 1) - lax.broadcasted_iota(jnp.int32, (tk, tq), 0)
        ramp_ref[...] = -slope * d.astype(F32)

    @pl.when(ki == 0)
    def _():
        m_ref[...] = jnp.full(m_ref.shape, NEG_INF, F32)
        l_ref[...] = jnp.zeros(l_ref.shape, F32)
        acc_ref[...] = jnp.zeros(acc_ref.shape, F32)

    q_lo = pos0 + qi * tq
    k_lo = ki * tk

    def raw_scores(c):
        return lax.dot_general(k_ref[:, c * dh:(c + 1) * dh], q_ref[:, c * dh:(c + 1) * dh],
                               (((1,), (1,)), ((), ())), preferred_element_type=F32)

    def online_update(c, t, cst):
        m_old = m_ref[c]
        m_new = jnp.maximum(m_old, jnp.max(t, axis=0, keepdims=True) + cst)
        alpha = jnp.exp2(m_old - m_new)
        p = jnp.exp2(t + (cst - m_new))
        l_ref[c] = alpha * l_ref[c] + jnp.sum(p, axis=0, keepdims=True)
        pv = lax.dot_general(v_ref[...], p.astype(BF16), (((0,), (0,)), ((), ())), preferred_element_type=F32)
        acc_ref[c] = alpha * acc_ref[c] + pv
        m_ref[c] = m_new

    all_past = k_lo + tk - 1 < q_lo

    @pl.when(all_past)
    def _():
        cst = -slope * (q_lo - k_lo).astype(F32)
        for c in range(2):
            online_update(c, raw_scores(c) + ramp_ref[...], cst)

    @pl.when(jnp.logical_not(all_past))
    def _():
        qpos = q_lo + lax.broadcasted_iota(jnp.int32, (1, tq), 1)
        kpos = k_lo + lax.broadcasted_iota(jnp.int32, (tk, 1), 0)
        visible = (jnp.right_shift(kpos, shift) <= jnp.right_shift(qpos, shift)) & (kpos < n_keys)
        bias = -slope * jnp.abs(qpos - kpos).astype(F32)
        for c in range(2):
            online_update(c, jnp.where(visible, raw_scores(c) + bias, NEG_INF), 0.0)

    @pl.when(last_ref[s] == 1)
    def _():
        o_ref[...] = _diff_out(acc_ref[0] / l_ref[0], acc_ref[1] / l_ref[1], _lambda(lam_ref, lam_init),
                               subg_ref[...], lam_init).astype(o_ref.dtype)


def _diff_attn(q_arr, q_off, k_arr, k_off, v_arr, v_off, n_keys, pos0, lam_params, subln_g, idx, lam_init,
               slopes):
    B, Tq, _ = q_arr.shape
    Tk = k_arr.shape[1]
    dh2 = subln_g.shape[1]
    H = N_HEADS
    tq = min(ATTN_Q_TILE, Tq)
    tk = Tk if Tk <= 2 * ATTN_K_TILE + LANES else ATTN_K_TILE
    assert Tq % tq == 0 and Tk % tk == 0 and pos0 + Tq <= n_keys <= Tk
    assert 1 << int(math.log2(CHUNK)) == CHUNK
    nk = Tk // tk
    qis, kis, lasts = [], [], []
    for qi in range(Tq // tq):
        last = min(_last_visible_tile(pos0 + (qi + 1) * tq - 1, tk), nk - 1)
        for ki in range(last + 1):
            qis.append(qi)
            kis.append(ki)
            lasts.append(int(ki == last))
    sched = [jnp.asarray(np.asarray(a, np.int32)) for a in (qis, kis, lasts)]

    body = functools.partial(_attn_body, pos0=pos0, n_keys=n_keys, lam_init=lam_init)
    grid_spec = pltpu.PrefetchScalarGridSpec(
        num_scalar_prefetch=3,
        grid=(B, H, len(qis)),
        in_specs=[pl.BlockSpec(memory_space=pltpu.SMEM),
                  pl.BlockSpec((None, 4, dh2 // 2), lambda b, h, s, qi, ki, la: (idx, 0, 0)),
                  pl.BlockSpec((None, dh2, 1), lambda b, h, s, qi, ki, la: (idx, 0, 0)),
                  pl.BlockSpec((None, tq, dh2), lambda b, h, s, qi, ki, la: (b, qi[s], q_off + h)),
                  pl.BlockSpec((None, tk, dh2), lambda b, h, s, qi, ki, la: (b, ki[s], k_off + h)),
                  pl.BlockSpec((None, tk, dh2), lambda b, h, s, qi, ki, la: (b, ki[s], v_off + h))],
        out_specs=pl.BlockSpec((None, tq, dh2), lambda b, h, s, qi, ki, la: (b, qi[s], h)),
        scratch_shapes=[pltpu.VMEM((2, 1, tq), F32), pltpu.VMEM((2, 1, tq), F32),
                        pltpu.VMEM((2, dh2, tq), F32), pltpu.VMEM((tk, tq), F32)])
    return pl.pallas_call(
        body,
        grid_spec=grid_spec,
        out_shape=jax.ShapeDtypeStruct((B, Tq, H * dh2), BF16),
        compiler_params=_cparams(("arbitrary", "arbitrary", "arbitrary")),
        name="diff_attn",
    )(*sched, slopes * LOG2E, lam_params, subln_g.reshape(subln_g.shape[0], dh2, 1), q_arr, k_arr, v_arr)


def _cached_attn_body(slope_ref, lam_ref, subg_ref, qkv_ref, ck1_ref, ck2_ref, cv1_ref, cv2_ref, o_ref,
                      *, past, lam_init):
    R = qkv_ref.shape[0]
    dh = ck1_ref.shape[1]
    dh2 = 2 * dh
    H = ck1_ref.shape[0] // past
    D = H * dh2
    shift = int(math.log2(CHUNK))
    trans_b = (((1,), (1,)), ((), ()))

    d_cache = (past + lax.broadcasted_iota(jnp.int32, (R, past), 0)
               - lax.broadcasted_iota(jnp.int32, (R, past), 1)).astype(F32)
    qpos = past + lax.broadcasted_iota(jnp.int32, (R, 1), 0)
    kpos = past + lax.broadcasted_iota(jnp.int32, (1, R), 1)
    vis_new = jnp.right_shift(kpos, shift) <= jnp.right_shift(qpos, shift)
    d_new = jnp.abs(qpos - kpos).astype(F32)
    lam = _lambda(lam_ref, lam_init)

    for h in range(H):
        slope = slope_ref[h]
        head_rows = pl.ds(h, past, stride=H)
        kc = (ck1_ref[head_rows, :].astype(BF16), ck2_ref[head_rows, :].astype(BF16))
        vc = jnp.concatenate([cv1_ref[head_rows, :].astype(BF16), cv2_ref[head_rows, :].astype(BF16)], axis=-1)
        q = qkv_ref[:, h * dh2:(h + 1) * dh2]
        kn = qkv_ref[:, D + h * dh2:D + (h + 1) * dh2]
        vn = qkv_ref[:, 2 * D + h * dh2:2 * D + (h + 1) * dh2]
        outs = []
        for c in range(2):
            qc = q[:, c * dh:(c + 1) * dh]
            t1 = lax.dot_general(qc, kc[c], trans_b, preferred_element_type=F32) - slope * d_cache
            t2 = lax.dot_general(qc, kn[:, c * dh:(c + 1) * dh], trans_b, preferred_element_type=F32) - slope * d_new
            t2 = jnp.where(vis_new, t2, NEG_INF)
            m = jnp.maximum(jnp.max(t1, axis=-1, keepdims=True), jnp.max(t2, axis=-1, keepdims=True))
            p1 = jnp.exp2(t1 - m)
            p2 = jnp.exp2(t2 - m)
            l = jnp.sum(p1, axis=-1, keepdims=True) + jnp.sum(p2, axis=-1, keepdims=True)
            outs.append((_dot(p1.astype(BF16), vc) + _dot(p2.astype(BF16), vn)) / l)
        o = outs[0] - lam * outs[1]
        r = o * lax.rsqrt(jnp.mean(o * o, axis=-1, keepdims=True) + SUBLN_EPS) * subg_ref[...]
        o_ref[:, h * dh2:(h + 1) * dh2] = (r * (1.0 - lam_init)).astype(o_ref.dtype)


def _cached_attn(qkv, cache_k, cache_v, idx, lam_params, subln_g, lam_init, slopes):
    B, R, _ = qkv.shape
    n_c, _, past, H, dh2 = cache_k.shape
    D = H * dh2
    body = functools.partial(_cached_attn_body, past=past, lam_init=lam_init)
    cspecs = [pl.BlockSpec((None, None, past * H, dh2 // 2), lambda b, half=half: (idx, b, 0, half))
              for half in (0, 1)]
    return pl.pallas_call(
        body,
        grid=(B,),
        in_specs=[pl.BlockSpec(memory_space=pltpu.SMEM),
                  pl.BlockSpec((None, 4, dh2 // 2), lambda b: (idx, 0, 0)),
                  pl.BlockSpec((None, 1, dh2), lambda b: (idx, 0, 0)),
                  pl.BlockSpec((None, R, 3 * D), lambda b: (b, 0, 0)),
                  ] + cspecs + cspecs,
        out_specs=pl.BlockSpec((None, R, D), lambda b: (b, 0, 0)),
        out_shape=jax.ShapeDtypeStruct((B, R, D), BF16),
        compiler_params=_cparams(("arbitrary",)),
        name="cached_attn",
    )(slopes * LOG2E, lam_params, subln_g.reshape(subln_g.shape[0], 1, dh2), qkv,
      *([cache_k.reshape(n_c, B, past * H, dh2)] * 2 + [cache_v.reshape(n_c, B, past * H, dh2)] * 2))


def _trunk(x, row0, mod, hist_a, hist_b, hist_k, hist_v, pos0, p, ffn_weights):
    G, R, D = x.shape
    depth = p["depth"]
    slopes = jnp.asarray([2.0 ** (-8.0 * (h + 1) / N_HEADS) for h in range(N_HEADS)], dtype=F32)
    new_a, new_b, new_k, new_v, copies = [], [], [], [], {}
    for i in range(depth):
        x, copies[i, 0] = _ffn(x, mod, row0, i, 0, p["g_pre"], p["g_post"], ffn_weights[i, 0])
        kind, idx = i % N_MIXERS, i // N_MIXERS
        if kind == 0:
            z = _pre_mm(x, mod, row0, i, p["g_pre"], p["sc_w_in"], idx)
            y, st = _conv("a", z, hist_a[idx], p["sc_w_conv"], idx)
            new_a.append(st)
            x = _mm_post(y, x, mod, row0, i, p["g_post"], p["sc_w_out"], idx)
        elif kind == 1:
            z = _pre_mm(x, mod, row0, i, p["g_pre"], p["cm_w_pw1"], idx, bias=p["cm_b_pw1"])
            y, st = _conv("b", z, hist_b[idx], p["cm_w_dw"], idx, bdw=p["cm_b_dw"])
            new_b.append(st)
            x = _mm_post(y, x, mod, row0, i, p["g_post"], p["cm_w_pw2"], idx, bias=p["cm_b_pw2"],
                         ln=(p["cm_ln_g"], p["cm_ln_b"]))
        else:
            lam_init = 0.8 - 0.6 * math.exp(-0.3 * i)
            dh2 = p["attn_subln_g"].shape[1]
            qkv, k_rows, v_rows = _qkv_proj(x, mod, row0, i, p["g_pre"], p["attn_w_qkv"], idx,
                                            (dh2 // 2) ** -0.5 * LOG2E)
            new_k.append(k_rows)
            new_v.append(v_rows)
            nblk = D // dh2
            if hist_k is None:
                o = _diff_attn(qkv, 0, qkv, nblk, qkv, 2 * nblk, R, pos0, p["attn_lambda"], p["attn_subln_g"],
                               idx, lam_init, slopes)
            else:
                assert pos0 == hist_k.shape[2]
                o = _cached_attn(qkv, hist_k, hist_v, idx, p["attn_lambda"], p["attn_subln_g"], lam_init, slopes)
            x = _mm_post(o, x, mod, row0, i, p["g_post"], p["attn_w_o"], idx)
        x, copies[i, 1] = _ffn(x, mod, row0, i, 2, p["g_pre"], p["g_post"], ffn_weights[i, 1])
    return (x, jnp.stack(new_a), jnp.stack(new_b), jnp.stack(new_k), jnp.stack(new_v)), copies


def kernel(x_prompt, x_sample, state_conv_a, state_conv_b, cache_k, cache_v, c_prompt, c_sample, w_mod, b_mod, g_pre, g_post, ffn_w_gu, ffn_w_down, sc_w_in, sc_w_conv, sc_w_out, cm_w_pw1, cm_b_pw1, cm_w_dw, cm_b_dw, cm_ln_g, cm_ln_b, cm_w_pw2, cm_b_pw2, attn_w_qkv, attn_lambda, attn_subln_g, attn_w_o):
    Bp, T, D = x_prompt.shape
    Bs = x_sample.shape[0]
    depth = g_pre.shape[0]
    n_a, n_b = state_conv_a.shape[0], state_conv_b.shape[0]
    assert Bp + Bs <= SEQ_PAD
    H, dh2 = N_HEADS, attn_subln_g.shape[1]

    c_all = jnp.concatenate([c_prompt, c_sample, jnp.zeros((SEQ_PAD - Bp - Bs, D), F32)], axis=0)
    mod = _adaln_mod(c_all, w_mod, b_mod)

    p = dict(
        depth=depth,
        g_pre=g_pre.reshape(depth * N_SUB, 1, D), g_post=g_post.reshape(depth * N_SUB, 1, D),
        sc_w_in=sc_w_in.astype(BF16), sc_w_conv=sc_w_conv, sc_w_out=sc_w_out.astype(BF16),
        cm_w_pw1=cm_w_pw1.astype(BF16), cm_b_pw1=cm_b_pw1, cm_w_dw=cm_w_dw, cm_b_dw=cm_b_dw,
        cm_ln_g=cm_ln_g, cm_ln_b=cm_ln_b, cm_w_pw2=cm_w_pw2.astype(BF16), cm_b_pw2=cm_b_pw2,
        attn_w_qkv=attn_w_qkv.astype(BF16), attn_lambda=attn_lambda, attn_subln_g=attn_subln_g,
        attn_w_o=attn_w_o.astype(BF16))

    past = cache_k.shape[2]
    f32_weights = {(i, f): ("f32", ffn_w_gu, ffn_w_down, f) for i in range(depth) for f in range(2)}
    (y_s, a_s, b_s, k_s, v_s), copies = _trunk(x_sample, Bp, mod, state_conv_a, state_conv_b, cache_k, cache_v,
                                               past, p, f32_weights)
    bf16_weights = {key: ("bf16",) + copies[key] for key in copies}
    zeros_a = jnp.zeros((n_a, Bp) + state_conv_a.shape[2:], F32)
    zeros_b = jnp.zeros((n_b, Bp) + state_conv_b.shape[2:], F32)
    (y_p, a_p, b_p, k_p, v_p), _ = _trunk(x_prompt, 0, mod, zeros_a, zeros_b, None, None, 0, p, bf16_weights)

    def heads(a):
        return a.reshape(a.shape[:3] + (H, dh2))
    return (y_p, y_s, a_p, a_s, b_p, b_s, heads(k_p), heads(v_p), heads(k_s), heads(v_s))
```

```python
import functools
import math

import numpy as np
import jax
import jax.numpy as jnp
from jax import lax
from jax.experimental import pallas as pl
from jax.experimental.pallas import tpu as pltpu

F32 = jnp.float32
BF16 = jnp.bfloat16

N_HEADS = 8
CHUNK = 64
FFN_RES = 0.5
NORM_EPS = 1e-6
SUBLN_EPS = 1e-5
LN_EPS = 1e-5
NEG_INF = -1e30
LOG2E = math.log2(math.e)
N_SUB = 3
N_MIXERS = 3

SEQ_PAD = 16
V7X_VMEM_LIMIT_BYTES = 56 * 1024 * 1024
V7X_VMEM_LIMIT_FFN_BYTES = 60 * 1024 * 1024
SUBLANES = 8
LANES = 128
ROW_BLK = 16
ROW_UNROLL = 4

FFN_ROWS = 1024
FFN_FF_TILE = 512
FFN_DOWN_TILE = 512
MM_ROWS = 512
QKV_ROWS = 1024
PRE_N_TILE = 1024
FRONT_ROWS = 1024
FRONT_COLS_A = 512
FRONT_COLS_B = 256
CONV_SUB_ROWS = 64
ATTN_Q_TILE = 1024
ATTN_K_TILE = 512


def _cparams(sem, vmem_limit_bytes=V7X_VMEM_LIMIT_BYTES):
    return pltpu.CompilerParams(dimension_semantics=sem, vmem_limit_bytes=vmem_limit_bytes)


def _row_blocks(G, R, rows):
    if R >= rows:
        assert R % rows == 0
        return 1, rows
    gb = min(G, rows // R)
    assert G % gb == 0 and R % ROW_BLK == 0
    return gb, R


def _silu(x):
    return x * jax.nn.sigmoid(x)


def _dot(a, b):
    return jnp.dot(a, b, preferred_element_type=F32)


def _mod_row(ref, seq0, b):
    return ref[pl.ds(seq0 + b, 1), :]


def _row_loop(n_rows, fn, static=False):
    nblk = n_rows // ROW_BLK
    if static:
        for i in range(nblk):
            fn(i * ROW_BLK)
        return

    def step(i, carry):
        fn(pl.multiple_of(i * ROW_BLK, ROW_BLK))
        return carry
    lax.fori_loop(0, nblk, step, 0, unroll=min(ROW_UNROLL, nblk))


def _set_vec(vec_ref, k, row):
    vec_ref[k] = jnp.broadcast_to(row, vec_ref.shape[1:])


def _prenorm_rows(x_ref, h_ref, vec_ref, gpre_ref, sh_ref, sc_ref, seq0, zero_ref=None):
    gb, rb, _ = x_ref.shape
    for b in range(gb):
        _set_vec(vec_ref, 0, gpre_ref[...] * (1.0 + _mod_row(sc_ref, seq0, b)))
        _set_vec(vec_ref, 1, _mod_row(sh_ref, seq0, b))

        def blk(r0, b=b):
            x = x_ref[b, r0:r0 + ROW_BLK, :]
            r = lax.rsqrt(jnp.mean(x * x, axis=-1, keepdims=True) + NORM_EPS)
            h_ref[b * rb + r0:b * rb + r0 + ROW_BLK, :] = ((x * r) * vec_ref[0] + vec_ref[1]).astype(BF16)
            if zero_ref is not None:
                zero_ref[b, r0:r0 + ROW_BLK, :] = jnp.zeros((ROW_BLK, zero_ref.shape[2]), F32)
        _row_loop(rb, blk, static=True)


def _postnorm_rows(o_ref, x_ref, vec_ref, gpost_ref, gt_ref, seq0, rw):
    gb, rb, _ = x_ref.shape
    for b in range(gb):
        _set_vec(vec_ref, 0, (rw * _mod_row(gt_ref, seq0, b)) * gpost_ref[...])

        def blk(r0, b=b):
            o = o_ref[b, r0:r0 + ROW_BLK, :]
            r = lax.rsqrt(jnp.mean(o * o, axis=-1, keepdims=True) + NORM_EPS)
            o_ref[b, r0:r0 + ROW_BLK, :] = x_ref[b, r0:r0 + ROW_BLK, :] + (o * r) * vec_ref[0]
        _row_loop(rb, blk, static=True)


def _adaln_body(c_ref, w_ref, b_ref, o_ref):
    a = _silu(c_ref[...]).astype(BF16)
    o_ref[...] = _dot(a, w_ref[...].astype(BF16)) + b_ref[...]


def _adaln_mod(c_all, w_mod, b_mod):
    L, D, N = w_mod.shape
    nm = N // D
    P = c_all.shape[0]
    return pl.pallas_call(
        _adaln_body,
        grid=(L, nm),
        in_specs=[pl.BlockSpec((P, D), lambda l, m: (0, 0)),
                  pl.BlockSpec((None, D, D), lambda l, m: (l, 0, m)),
                  pl.BlockSpec((None, 1, D), lambda l, m: (l * nm + m, 0, 0))],
        out_specs=pl.BlockSpec((None, P, D), lambda l, m: (l * nm + m, 0, 0)),
        out_shape=jax.ShapeDtypeStruct((L * nm, P, D), F32),
        compiler_params=_cparams(("arbitrary", "arbitrary")),
        name="adaln_mod",
    )(c_all, w_mod, b_mod.reshape(L * nm, 1, D))


def _mod_specs(layer, sub, kinds, D):
    specs = []
    for k in kinds:
        idx = layer * 3 * N_SUB + 3 * sub + k
        specs.append(pl.BlockSpec((None, SEQ_PAD, D), lambda *_, idx=idx: (idx, 0, 0)))
    return specs


def _ffn_body(x_ref, sh_ref, sc_ref, gt_ref, gpre_ref, gpost_ref, wg_ref, wu_ref, wd_ref, *rest,
              row0, n_chunks, emit):
    if emit:
        o_ref, wgo_ref, wuo_ref, wdo_ref, h_ref, vec_ref = rest
    else:
        o_ref, h_ref, vec_ref = rest
    j = pl.program_id(2)
    gb, rb, _ = x_ref.shape
    seq0 = row0 + pl.program_id(0) * gb

    @pl.when(j == 0)
    def _():
        _prenorm_rows(x_ref, h_ref, vec_ref, gpre_ref, sh_ref, sc_ref, seq0, zero_ref=o_ref)

    if emit:
        wg, wu, wd = wg_ref[...].astype(BF16), wu_ref[...].astype(BF16), wd_ref[...].astype(BF16)
        wgo_ref[...] = wg
        wuo_ref[...] = wu
        wdo_ref[...] = wd
    else:
        wg, wu, wd = wg_ref[...], wu_ref[...], wd_ref

    h = h_ref[...]
    act = (_silu(_dot(h, wg)) * _dot(h, wu)).astype(BF16)
    dn = min(FFN_DOWN_TILE, o_ref.shape[2])
    for n0 in range(0, o_ref.shape[2], dn):
        y = _dot(act, wd[:, n0:n0 + dn])
        for b in range(gb):
            o_ref[b, :, n0:n0 + dn] += y[b * rb:(b + 1) * rb, :]

    @pl.when(j == n_chunks - 1)
    def _():
        _postnorm_rows(o_ref, x_ref, vec_ref, gpost_ref, gt_ref, seq0, FFN_RES)


def _ffn(x, mod, row0, layer, sub, g_pre, g_post, weights):
    G, R, D = x.shape
    emit = weights[0] == "f32"
    tf = FFN_FF_TILE
    gb, rb = _row_blocks(G, R, FFN_ROWS)
    grid_rows = (G // gb, R // rb)
    if emit:
        _, w_gu, w_down, ffn_idx = weights
        dff = w_down.shape[2]
        n_chunks = dff // tf
        assert grid_rows == (1, 1)
        wspecs = [pl.BlockSpec((None, None, D, tf), lambda g, t, j: (layer, ffn_idx, 0, j)),
                  pl.BlockSpec((None, None, D, tf), lambda g, t, j: (layer, ffn_idx, 0, j + n_chunks)),
                  pl.BlockSpec((None, None, tf, D), lambda g, t, j: (layer, ffn_idx, j, 0))]
        wargs = [w_gu, w_gu, w_down]
    else:
        _, wg, wu, wd = weights
        dff = wd.shape[0]
        n_chunks = dff // tf
        wargs = [wg, wu, wd]
    copy_specs = [pl.BlockSpec((D, tf), lambda g, t, j: (0, j)),
                  pl.BlockSpec((D, tf), lambda g, t, j: (0, j)),
                  pl.BlockSpec((tf, D), lambda g, t, j: (j, 0))]
    if not emit:
        wspecs = copy_specs
    gidx = layer * N_SUB + sub
    body = functools.partial(_ffn_body, row0=row0, n_chunks=n_chunks, emit=emit)
    xspec = pl.BlockSpec((gb, rb, D), lambda g, t, j: (g, t, 0))
    out_specs, out_shape = [xspec], [jax.ShapeDtypeStruct((G, R, D), F32)]
    if emit:
        out_specs += copy_specs
        out_shape += [jax.ShapeDtypeStruct((D, dff), BF16), jax.ShapeDtypeStruct((D, dff), BF16),
                      jax.ShapeDtypeStruct((dff, D), BF16)]
    outs = pl.pallas_call(
        body,
        grid=grid_rows + (n_chunks,),
        in_specs=[xspec] + _mod_specs(layer, sub, (0, 1, 2), D) + [
            pl.BlockSpec((None, 1, D), lambda g, t, j: (gidx, 0, 0)),
            pl.BlockSpec((None, 1, D), lambda g, t, j: (gidx, 0, 0))] + wspecs,
        out_specs=out_specs,
        out_shape=out_shape,
        scratch_shapes=[pltpu.VMEM((gb * rb, D), BF16), pltpu.VMEM((2, ROW_BLK, D), F32)],
        compiler_params=_cparams(("arbitrary", "arbitrary", "arbitrary"), V7X_VMEM_LIMIT_FFN_BYTES),
        name="ffn_cast" if emit else "ffn",
    )(x, mod, mod, mod, g_pre, g_post, *wargs)
    return (outs[0], tuple(outs[1:])) if emit else (outs[0], None)


def _qkv_body(x_ref, sh_ref, sc_ref, gpre_ref, w_ref, qkv_ref, k_ref, v_ref, h_ref, vec_ref, *, row0, nb, q_scale):
    n = pl.program_id(2)
    gb, rb, _ = x_ref.shape
    seq0 = row0 + pl.program_id(0) * gb

    @pl.when(n == 0)
    def _():
        _prenorm_rows(x_ref, h_ref, vec_ref, gpre_ref, sh_ref, sc_ref, seq0)

    z = _dot(h_ref[...], w_ref[...])
    zs = z * jnp.where(n < nb, q_scale, 1.0)
    for b in range(gb):
        qkv_ref[b] = zs[b * rb:(b + 1) * rb, :].astype(BF16)

    @pl.when((n >= nb) & (n < 2 * nb))
    def _():
        for b in range(gb):
            k_ref[b] = z[b * rb:(b + 1) * rb, :]

    @pl.when(n >= 2 * nb)
    def _():
        for b in range(gb):
            v_ref[b] = z[b * rb:(b + 1) * rb, :]


def _qkv_proj(x, mod, row0, layer, g_pre, w, widx, q_scale):
    G, R, D = x.shape
    tn = PRE_N_TILE
    nb = D // tn
    gb, rb = _row_blocks(G, R, QKV_ROWS)
    gidx = layer * N_SUB + 1
    body = functools.partial(_qkv_body, row0=row0, nb=nb, q_scale=q_scale)
    kspec = pl.BlockSpec((gb, rb, tn), lambda g, t, n: (g, t, jnp.clip(n - nb, 0, nb - 1)))
    vspec = pl.BlockSpec((gb, rb, tn), lambda g, t, n: (g, t, jnp.clip(n - 2 * nb, 0, nb - 1)))
    return pl.pallas_call(
        body,
        grid=(G // gb, R // rb, 3 * nb),
        in_specs=[pl.BlockSpec((gb, rb, D), lambda g, t, n: (g, t, 0))] + _mod_specs(layer, 1, (0, 1), D) + [
            pl.BlockSpec((None, 1, D), lambda g, t, n: (gidx, 0, 0)),
            pl.BlockSpec((None, D, tn), lambda g, t, n: (widx, 0, n))],
        out_specs=[pl.BlockSpec((gb, rb, tn), lambda g, t, n: (g, t, n)), kspec, vspec],
        out_shape=[jax.ShapeDtypeStruct((G, R, 3 * D), BF16), jax.ShapeDtypeStruct((G, R, D), F32),
                   jax.ShapeDtypeStruct((G, R, D), F32)],
        scratch_shapes=[pltpu.VMEM((gb * rb, D), BF16), pltpu.VMEM((2, ROW_BLK, D), F32)],
        compiler_params=_cparams(("arbitrary", "arbitrary", "arbitrary")),
        name="qkv_proj",
    )(x, mod, mod, g_pre, w)


def _mm_post_body(y_ref, w_ref, x_ref, gt_ref, gpost_ref, *rest, row0, has_bias, has_ln):
    rest = list(rest)
    b_ref = rest.pop(0) if has_bias else None
    lng_ref, lnb_ref = (rest.pop(0), rest.pop(0)) if has_ln else (None, None)
    o_ref = rest.pop(0)
    a_ref = rest.pop(0) if has_ln else None
    vec_ref = rest.pop(0)
    gb, rb, _ = x_ref.shape
    seq0 = row0 + pl.program_id(0) * gb

    if has_ln:
        for b in range(gb):
            def blk(r0, b=b):
                y = y_ref[b, pl.ds(r0, ROW_BLK), :]
                yc = y - jnp.mean(y, axis=-1, keepdims=True)
                n = yc * lax.rsqrt(jnp.mean(yc * yc, axis=-1, keepdims=True) + LN_EPS) * lng_ref[...] + lnb_ref[...]
                a_ref[pl.ds(b * rb + r0, ROW_BLK), :] = _silu(n).astype(BF16)
            _row_loop(rb, blk)
        a = a_ref[...]
    else:
        a = y_ref[...].reshape(gb * rb, y_ref.shape[2])

    o = _dot(a, w_ref[...])
    if has_bias:
        o = o + b_ref[...]
    for b in range(gb):
        o_ref[b] = o[b * rb:(b + 1) * rb, :]
    _postnorm_rows(o_ref, x_ref, vec_ref, gpost_ref, gt_ref, seq0, 1.0)


def _mm_post(y, x, mod, row0, layer, g_post, w, widx, bias=None, ln=None):
    G, R, D = x.shape
    Din = y.shape[2]
    gb, rb = _row_blocks(G, R, MM_ROWS)
    gidx = layer * N_SUB + 1
    has_bias, has_ln = bias is not None, ln is not None
    assert has_ln or y.dtype == BF16
    body = functools.partial(_mm_post_body, row0=row0, has_bias=has_bias, has_ln=has_ln)
    xspec = pl.BlockSpec((gb, rb, D), lambda g, t: (g, t, 0))
    in_specs = [pl.BlockSpec((gb, rb, Din), lambda g, t: (g, t, 0)),
                pl.BlockSpec((None, Din, D), lambda g, t: (widx, 0, 0)),
                xspec] + _mod_specs(layer, 1, (2,), D) + [
        pl.BlockSpec((None, 1, D), lambda g, t: (gidx, 0, 0))]
    args = [y, w, x, mod, g_post]
    if has_bias:
        in_specs.append(pl.BlockSpec((None, 1, D), lambda g, t: (widx, 0, 0)))
        args.append(bias.reshape(bias.shape[0], 1, D))
    if has_ln:
        for p in ln:
            in_specs.append(pl.BlockSpec((None, 1, Din), lambda g, t: (widx, 0, 0)))
            args.append(p.reshape(p.shape[0], 1, Din))
    scratch = ([pltpu.VMEM((gb * rb, Din), BF16)] if has_ln else []) + [pltpu.VMEM((2, ROW_BLK, D), F32)]
    return pl.pallas_call(
        body,
        grid=(G // gb, R // rb),
        in_specs=in_specs,
        out_specs=xspec,
        out_shape=jax.ShapeDtypeStruct((G, R, D), F32),
        scratch_shapes=scratch,
        compiler_params=_cparams(("arbitrary", "arbitrary")),
        name="mm_post",
    )(*args)


def _conv_stage(ext_ref, carry_ref, hist_ref, nh_ref, b, c, t, new_rows, W):
    rb = new_rows.shape[0]
    pad = ext_ref.shape[1] - rb
    lo = pad - (W - 1)

    @pl.when(t == 0)
    def _():
        ext_ref[b, lo:pad, :] = hist_ref[b]

    @pl.when(t > 0)
    def _():
        ext_ref[b, lo:pad, :] = carry_ref[c, b, lo:pad, :]

    ext_ref[b, pad:pad + rb, :] = new_rows
    tail = ext_ref[b, lo + rb:pad + rb, :]
    carry_ref[c, b, lo:pad, :] = tail
    tc = ext_ref.shape[2]
    nh_ref[b, :, pl.ds(pl.multiple_of(c * tc, tc), tc)] = tail
    return lo


def _front_a_body(x_ref, sh_ref, sc_ref, gpre_ref, wb_ref, wc_ref, wx_ref, hist_ref, wconv_ref,
                  y_ref, nh_ref, h_ref, vec_ref, ext_ref, carry_ref, *, row0):
    t, c = pl.program_id(1), pl.program_id(2)
    gb, rb, _ = x_ref.shape
    W = wconv_ref.shape[0]
    seq0 = row0 + pl.program_id(0) * gb

    @pl.when(c == 0)
    def _():
        _prenorm_rows(x_ref, h_ref, vec_ref, gpre_ref, sh_ref, sc_ref, seq0)

    h = h_ref[...]
    zb, zc, zx = _dot(h, wb_ref[...]), _dot(h, wc_ref[...]), _dot(h, wx_ref[...])
    rs = min(CONV_SUB_ROWS, rb)
    for b in range(gb):
        lo = _conv_stage(ext_ref, carry_ref, hist_ref, nh_ref, b, c, t,
                         zc[b * rb:(b + 1) * rb] * zx[b * rb:(b + 1) * rb], W)
        for r0 in range(0, rb, rs):
            acc = wconv_ref[0:1, :] * ext_ref[b, lo + r0:lo + r0 + rs, :]
            for j in range(1, W):
                acc = acc + wconv_ref[j:j + 1, :] * ext_ref[b, lo + j + r0:lo + j + r0 + rs, :]
            y_ref[b, r0:r0 + rs, :] = (zb[b * rb + r0:b * rb + r0 + rs] * acc).astype(y_ref.dtype)


def _front_b_body(x_ref, sh_ref, sc_ref, gpre_ref, wa_ref, wg_ref, ba_ref, bg_ref, hist_ref, wconv_ref, bdw_ref,
                  y_ref, nh_ref, h_ref, vec_ref, ext_ref, carry_ref, ph_ref, *, row0):
    t, c = pl.program_id(1), pl.program_id(2)
    gb, rb, _ = x_ref.shape
    W = wconv_ref.shape[0]
    seq0 = row0 + pl.program_id(0) * gb

    @pl.when(c == 0)
    def _():
        _prenorm_rows(x_ref, h_ref, vec_ref, gpre_ref, sh_ref, sc_ref, seq0)

    h = h_ref[...]
    u = (_dot(h, wa_ref[...]) + ba_ref[...]) * jax.nn.sigmoid(_dot(h, wg_ref[...]) + bg_ref[...])
    rs = min(CONV_SUB_ROWS, rb)
    n_shift = ph_ref.shape[1]
    for b in range(gb):
        lo = _conv_stage(ext_ref, carry_ref, hist_ref, nh_ref, b, c, t, u[b * rb:(b + 1) * rb], W)
        for p in range(1, SUBLANES):
            ph_ref[p - 1] = ext_ref[b, p:p + n_shift, :]

        def sub_block(i, carry, b=b, lo=lo):
            r0 = pl.multiple_of(i * rs, rs)
            acc = None
            for j in range(W):
                a, p = divmod(lo + j, SUBLANES)
                start = pl.multiple_of(r0 + a * SUBLANES, SUBLANES)
                win = ext_ref[b, pl.ds(start, rs), :] if p == 0 else ph_ref[p - 1, pl.ds(start, rs), :]
                term = wconv_ref[j:j + 1, :] * win
                acc = term if acc is None else acc + term
            y_ref[b, pl.ds(r0, rs), :] = acc + bdw_ref[...]
            return carry
        lax.fori_loop(0, rb // rs, sub_block, 0)


def _mixer_front(kind, x, mod, row0, layer, g_pre, w, widx, hist, wconv, bias=None, bdw=None):
    G, R, D = x.shape
    nsplit = 3 if kind == "a" else 2
    W = wconv.shape[1]
    gb, rb = _row_blocks(G, R, FRONT_ROWS)
    assert rb >= W - 1 and rb % min(CONV_SUB_ROWS, rb) == 0
    tc = FRONT_COLS_A if kind == "a" else FRONT_COLS_B
    nc = D // tc
    pad = -(-(W - 1) // SUBLANES) * SUBLANES
    gidx = layer * N_SUB + 1
    wspecs = [pl.BlockSpec((None, D, tc), lambda g, t, c, s=s: (widx, 0, s * nc + c)) for s in range(nsplit)]
    hist_spec = pl.BlockSpec((gb, W - 1, tc), lambda g, t, c: (g, 0, c))
    in_specs = [pl.BlockSpec((gb, rb, D), lambda g, t, c: (g, t, 0))] + _mod_specs(layer, 1, (0, 1), D) + [
        pl.BlockSpec((None, 1, D), lambda g, t, c: (gidx, 0, 0))] + wspecs
    args = [x, mod, mod, g_pre] + [w] * nsplit
    scratch = [pltpu.VMEM((gb * rb, D), BF16), pltpu.VMEM((2, ROW_BLK, D), F32),
               pltpu.VMEM((gb, pad + rb, tc), F32), pltpu.VMEM((nc, gb, pad, tc), F32)]
    if kind == "a":
        body, out_dtype = _front_a_body, BF16
    else:
        body, out_dtype = _front_b_body, F32
        in_specs += [pl.BlockSpec((None, 1, tc), lambda g, t, c, s=s: (widx, 0, s * nc + c)) for s in range(nsplit)]
        args += [bias.reshape(bias.shape[0], 1, nsplit * D)] * nsplit
        scratch.append(pltpu.VMEM((SUBLANES - 1, pad + rb - SUBLANES, tc), F32))
    in_specs += [hist_spec, pl.BlockSpec((None, W, tc), lambda g, t, c: (widx, 0, c))]
    args += [hist, wconv]
    if kind == "b":
        in_specs.append(pl.BlockSpec((None, 1, tc), lambda g, t, c: (widx, 0, c)))
        args.append(bdw.reshape(bdw.shape[0], 1, D))
    return pl.pallas_call(
        functools.partial(body, row0=row0),
        grid=(G // gb, R // rb, nc),
        in_specs=in_specs,
        out_specs=[pl.BlockSpec((gb, rb, tc), lambda g, t, c: (g, t, c)),
                   pl.BlockSpec((gb, W - 1, D), lambda g, t, c: (g, 0, 0))],
        out_shape=[jax.ShapeDtypeStruct((G, R, D), out_dtype),
                   jax.ShapeDtypeStruct((G, W - 1, D), F32)],
        scratch_shapes=scratch,
        compiler_params=_cparams(("arbitrary", "arbitrary", "arbitrary")),
        name="front_" + kind,
    )(*args)


def _last_visible_tile(q_pos, tk):
    return ((q_pos // CHUNK) * CHUNK + CHUNK - 1) // tk


def _lambda(lam_ref, lam_init):
    lp = lam_ref[...]
    return (jnp.exp(jnp.sum(lp[0:1] * lp[1:2], axis=-1, keepdims=True))
            - jnp.exp(jnp.sum(lp[2:3] * lp[3:4], axis=-1, keepdims=True)) + lam_init)


def _diff_out(o1, o2, lam, subg, lam_init):
    o = o1 - lam * o2
    r = o * lax.rsqrt(jnp.mean(o * o, axis=0, keepdims=True) + SUBLN_EPS) * subg
    return (r * (1.0 - lam_init)).T


def _attn_body(qi_ref, ki_ref, last_ref, slope_ref, lam_ref, subg_ref, q_ref, k_ref, v_ref, o_ref,
               m_ref, l_ref, acc_ref, ramp_ref, *, pos0, n_keys, lam_init):
    h, s = pl.program_id(1), pl.program_id(2)
    qi, ki = qi_ref[s], ki_ref[s]
    tq, dh2 = q_ref.shape
    tk = k_ref.shape[0]
    dh = dh2 // 2
    shift = int(math.log2(CHUNK))
    slope = slope_ref[h]

    @pl.when(s == 0)
    def _():
        d = lax.broadcasted_iota(jnp.int32, (tk, tq), 1) - lax.broadcasted_iota(jnp.int32, (tk, tq), 0)
        ramp_ref[...] = -slope * d.astype(F32)

    @pl.when(ki == 0)
    def _():
        m_ref[...] = jnp.full(m_ref.shape, NEG_INF, F32)
        l_ref[...] = jnp.zeros(l_ref.shape, F32)
        acc_ref[...] = jnp.zeros(acc_ref.shape, F32)

    q_lo = pos0 + qi * tq
    k_lo = ki * tk

    def raw_scores(c):
        return lax.dot_general(k_ref[:, c * dh:(c + 1) * dh], q_ref[:, c * dh:(c + 1) * dh],
                               (((1,), (1,)), ((), ())), preferred_element_type=F32)

    def online_update(c, t, cst):
        m_old = m_ref[c]
        m_new = jnp.maximum(m_old, jnp.max(t, axis=0, keepdims=True) + cst)
        alpha = jnp.exp2(m_old - m_new)
        p = jnp.exp2(t + (cst - m_new))
        l_ref[c] = alpha * l_ref[c] + jnp.sum(p, axis=0, keepdims=True)
        pv = lax.dot_general(v_ref[...], p.astype(BF16), (((0,), (0,)), ((), ())), preferred_element_type=F32)
        acc_ref[c] = alpha * acc_ref[c] + pv
        m_ref[c] = m_new

    all_past = k_lo + tk - 1 < q_lo

    @pl.when(all_past)
    def _():
        cst = -slope * (q_lo - k_lo).astype(F32)
        for c in range(2):
            online_update(c, raw_scores(c) + ramp_ref[...], cst)

    @pl.when(jnp.logical_not(all_past))
    def _():
        qpos = q_lo + lax.broadcasted_iota(jnp.int32, (1, tq), 1)
        kpos = k_lo + lax.broadcasted_iota(jnp.int32, (tk, 1), 0)
        visible = (jnp.right_shift(kpos, shift) <= jnp.right_shift(qpos, shift)) & (kpos < n_keys)
        bias = -slope * jnp.abs(qpos - kpos).astype(F32)
        for c in range(2):
            online_update(c, jnp.where(visible, raw_scores(c) + bias, NEG_INF), 0.0)

    @pl.when(last_ref[s] == 1)
    def _():
        o_ref[...] = _diff_out(acc_ref[0] / l_ref[0], acc_ref[1] / l_ref[1], _lambda(lam_ref, lam_init),
                               subg_ref[...], lam_init).astype(o_ref.dtype)


def _diff_attn(q_arr, q_off, k_arr, k_off, v_arr, v_off, n_keys, pos0, lam_params, subln_g, idx, lam_init,
               slopes):
    B, Tq, _ = q_arr.shape
    Tk = k_arr.shape[1]
    dh2 = subln_g.shape[1]
    H = N_HEADS
    tq = min(ATTN_Q_TILE, Tq)
    tk = Tk if Tk <= 2 * ATTN_K_TILE + LANES else ATTN_K_TILE
    assert Tq % tq == 0 and Tk % tk == 0 and pos0 + Tq <= n_keys <= Tk
    assert 1 << int(math.log2(CHUNK)) == CHUNK
    nk = Tk // tk
    qis, kis, lasts = [], [], []
    for qi in range(Tq // tq):
        last = min(_last_visible_tile(pos0 + (qi + 1) * tq - 1, tk), nk - 1)
        for ki in range(last + 1):
            qis.append(qi)
            kis.append(ki)
            lasts.append(int(ki == last))
    sched = [jnp.asarray(np.asarray(a, np.int32)) for a in (qis, kis, lasts)]

    body = functools.partial(_attn_body, pos0=pos0, n_keys=n_keys, lam_init=lam_init)
    grid_spec = pltpu.PrefetchScalarGridSpec(
        num_scalar_prefetch=3,
        grid=(B, H, len(qis)),
        in_specs=[pl.BlockSpec(memory_space=pltpu.SMEM),
                  pl.BlockSpec((None, 4, dh2 // 2), lambda b, h, s, qi, ki, la: (idx, 0, 0)),
                  pl.BlockSpec((None, dh2, 1), lambda b, h, s, qi, ki, la: (idx, 0, 0)),
                  pl.BlockSpec((None, tq, dh2), lambda b, h, s, qi, ki, la: (b, qi[s], q_off + h)),
                  pl.BlockSpec((None, tk, dh2), lambda b, h, s, qi, ki, la: (b, ki[s], k_off + h)),
                  pl.BlockSpec((None, tk, dh2), lambda b, h, s, qi, ki, la: (b, ki[s], v_off + h))],
        out_specs=pl.BlockSpec((None, tq, dh2), lambda b, h, s, qi, ki, la: (b, qi[s], h)),
        scratch_shapes=[pltpu.VMEM((2, 1, tq), F32), pltpu.VMEM((2, 1, tq), F32),
                        pltpu.VMEM((2, dh2, tq), F32), pltpu.VMEM((tk, tq), F32)])
    return pl.pallas_call(
        body,
        grid_spec=grid_spec,
        out_shape=jax.ShapeDtypeStruct((B, Tq, H * dh2), BF16),
        compiler_params=_cparams(("arbitrary", "arbitrary", "arbitrary")),
        name="diff_attn",
    )(*sched, slopes * LOG2E, lam_params, subln_g.reshape(subln_g.shape[0], dh2, 1), q_arr, k_arr, v_arr)


def _cached_attn_body(slope_ref, lam_ref, subg_ref, qkv_ref, ck1_ref, ck2_ref, cv1_ref, cv2_ref, o_ref,
                      *, past, lam_init):
    R = qkv_ref.shape[0]
    dh = ck1_ref.shape[1]
    dh2 = 2 * dh
    H = ck1_ref.shape[0] // past
    D = H * dh2
    shift = int(math.log2(CHUNK))
    trans_b = (((1,), (1,)), ((), ()))

    d_cache = (past + lax.broadcasted_iota(jnp.int32, (R, past), 0)
               - lax.broadcasted_iota(jnp.int32, (R, past), 1)).astype(F32)
    qpos = past + lax.broadcasted_iota(jnp.int32, (R, 1), 0)
    kpos = past + lax.broadcasted_iota(jnp.int32, (1, R), 1)
    vis_new = jnp.right_shift(kpos, shift) <= jnp.right_shift(qpos, shift)
    d_new = jnp.abs(qpos - kpos).astype(F32)
    lam = _lambda(lam_ref, lam_init)

    for h in range(H):
        slope = slope_ref[h]
        head_rows = pl.ds(h, past, stride=H)
        kc = (ck1_ref[head_rows, :].astype(BF16), ck2_ref[head_rows, :].astype(BF16))
        vc = jnp.concatenate([cv1_ref[head_rows, :].astype(BF16), cv2_ref[head_rows, :].astype(BF16)], axis=-1)
        q = qkv_ref[:, h * dh2:(h + 1) * dh2]
        kn = qkv_ref[:, D + h * dh2:D + (h + 1) * dh2]
        vn = qkv_ref[:, 2 * D + h * dh2:2 * D + (h + 1) * dh2]
        outs = []
        for c in range(2):
            qc = q[:, c * dh:(c + 1) * dh]
            t1 = lax.dot_general(qc, kc[c], trans_b, preferred_element_type=F32) - slope * d_cache
            t2 = lax.dot_general(qc, kn[:, c * dh:(c + 1) * dh], trans_b, preferred_element_type=F32) - slope * d_new
            t2 = jnp.where(vis_new, t2, NEG_INF)
            m = jnp.maximum(jnp.max(t1, axis=-1, keepdims=True), jnp.max(t2, axis=-1, keepdims=True))
            p1 = jnp.exp2(t1 - m)
            p2 = jnp.exp2(t2 - m)
            l = jnp.sum(p1, axis=-1, keepdims=True) + jnp.sum(p2, axis=-1, keepdims=True)
            outs.append((_dot(p1.astype(BF16), vc) + _dot(p2.astype(BF16), vn)) / l)
        o = outs[0] - lam * outs[1]
        r = o * lax.rsqrt(jnp.mean(o * o, axis=-1, keepdims=True) + SUBLN_EPS) * subg_ref[...]
        o_ref[:, h * dh2:(h + 1) * dh2] = (r * (1.0 - lam_init)).astype(o_ref.dtype)


def _cached_attn(qkv, cache_k, cache_v, idx, lam_params, subln_g, lam_init, slopes):
    B, R, _ = qkv.shape
    n_c, _, past, H, dh2 = cache_k.shape
    D = H * dh2
    body = functools.partial(_cached_attn_body, past=past, lam_init=lam_init)
    cspecs = [pl.BlockSpec((None, None, past * H, dh2 // 2), lambda b, half=half: (idx, b, 0, half))
              for half in (0, 1)]
    return pl.pallas_call(
        body,
        grid=(B,),
        in_specs=[pl.BlockSpec(memory_space=pltpu.SMEM),
                  pl.BlockSpec((None, 4, dh2 // 2), lambda b: (idx, 0, 0)),
                  pl.BlockSpec((None, 1, dh2), lambda b: (idx, 0, 0)),
                  pl.BlockSpec((None, R, 3 * D), lambda b: (b, 0, 0)),
                  ] + cspecs + cspecs,
        out_specs=pl.BlockSpec((None, R, D), lambda b: (b, 0, 0)),
        out_shape=jax.ShapeDtypeStruct((B, R, D), BF16),
        compiler_params=_cparams(("arbitrary",)),
        name="cached_attn",
    )(slopes * LOG2E, lam_params, subln_g.reshape(subln_g.shape[0], 1, dh2), qkv,
      *([cache_k.reshape(n_c, B, past * H, dh2)] * 2 + [cache_v.reshape(n_c, B, past * H, dh2)] * 2))


def _trunk(x, row0, mod, hist_a, hist_b, hist_k, hist_v, pos0, p, ffn_weights):
    G, R, D = x.shape
    depth = p["depth"]
    slopes = jnp.asarray([2.0 ** (-8.0 * (h + 1) / N_HEADS) for h in range(N_HEADS)], dtype=F32)
    new_a, new_b, new_k, new_v, copies = [], [], [], [], {}
    for i in range(depth):
        x, copies[i, 0] = _ffn(x, mod, row0, i, 0, p["g_pre"], p["g_post"], ffn_weights[i, 0])
        kind, idx = i % N_MIXERS, i // N_MIXERS
        if kind == 0:
            y, st = _mixer_front("a", x, mod, row0, i, p["g_pre"], p["sc_w_in"], idx, hist_a[idx], p["sc_w_conv"])
            new_a.append(st)
            x = _mm_post(y, x, mod, row0, i, p["g_post"], p["sc_w_out"], idx)
        elif kind == 1:
            y, st = _mixer_front("b", x, mod, row0, i, p["g_pre"], p["cm_w_pw1"], idx, hist_b[idx], p["cm_w_dw"],
                                 bias=p["cm_b_pw1"], bdw=p["cm_b_dw"])
            new_b.append(st)
            x = _mm_post(y, x, mod, row0, i, p["g_post"], p["cm_w_pw2"], idx, bias=p["cm_b_pw2"],
                         ln=(p["cm_ln_g"], p["cm_ln_b"]))
        else:
            lam_init = 0.8 - 0.6 * math.exp(-0.3 * i)
            dh2 = p["attn_subln_g"].shape[1]
            qkv, k_rows, v_rows = _qkv_proj(x, mod, row0, i, p["g_pre"], p["attn_w_qkv"], idx,
                                            (dh2 // 2) ** -0.5 * LOG2E)
            new_k.append(k_rows)
            new_v.append(v_rows)
            nblk = D // dh2
            if hist_k is None:
                o = _diff_attn(qkv, 0, qkv, nblk, qkv, 2 * nblk, R, pos0, p["attn_lambda"], p["attn_subln_g"],
                               idx, lam_init, slopes)
            else:
                assert pos0 == hist_k.shape[2]
                o = _cached_attn(qkv, hist_k, hist_v, idx, p["attn_lambda"], p["attn_subln_g"], lam_init, slopes)
            x = _mm_post(o, x, mod, row0, i, p["g_post"], p["attn_w_o"], idx)
        x, copies[i, 1] = _ffn(x, mod, row0, i, 2, p["g_pre"], p["g_post"], ffn_weights[i, 1])
    return (x, jnp.stack(new_a), jnp.stack(new_b), jnp.stack(new_k), jnp.stack(new_v)), copies


def kernel(x_prompt, x_sample, state_conv_a, state_conv_b, cache_k, cache_v, c_prompt, c_sample, w_mod, b_mod, g_pre, g_post, ffn_w_gu, ffn_w_down, sc_w_in, sc_w_conv, sc_w_out, cm_w_pw1, cm_b_pw1, cm_w_dw, cm_b_dw, cm_ln_g, cm_ln_b, cm_w_pw2, cm_b_pw2, attn_w_qkv, attn_lambda, attn_subln_g, attn_w_o):
    Bp, T, D = x_prompt.shape
    Bs = x_sample.shape[0]
    depth = g_pre.shape[0]
    n_a, n_b = state_conv_a.shape[0], state_conv_b.shape[0]
    assert Bp + Bs <= SEQ_PAD
    H, dh2 = N_HEADS, attn_subln_g.shape[1]

    c_all = jnp.concatenate([c_prompt, c_sample, jnp.zeros((SEQ_PAD - Bp - Bs, D), F32)], axis=0)
    mod = _adaln_mod(c_all, w_mod, b_mod)

    p = dict(
        depth=depth,
        g_pre=g_pre.reshape(depth * N_SUB, 1, D), g_post=g_post.reshape(depth * N_SUB, 1, D),
        sc_w_in=sc_w_in.astype(BF16), sc_w_conv=sc_w_conv, sc_w_out=sc_w_out.astype(BF16),
        cm_w_pw1=cm_w_pw1.astype(BF16), cm_b_pw1=cm_b_pw1, cm_w_dw=cm_w_dw, cm_b_dw=cm_b_dw,
        cm_ln_g=cm_ln_g, cm_ln_b=cm_ln_b, cm_w_pw2=cm_w_pw2.astype(BF16), cm_b_pw2=cm_b_pw2,
        attn_w_qkv=attn_w_qkv.astype(BF16), attn_lambda=attn_lambda, attn_subln_g=attn_subln_g,
        attn_w_o=attn_w_o.astype(BF16))

    past = cache_k.shape[2]
    f32_weights = {(i, f): ("f32", ffn_w_gu, ffn_w_down, f) for i in range(depth) for f in range(2)}
    (y_s, a_s, b_s, k_s, v_s), copies = _trunk(x_sample, Bp, mod, state_conv_a, state_conv_b, cache_k, cache_v,
                                               past, p, f32_weights)
    bf16_weights = {key: ("bf16",) + copies[key] for key in copies}
    zeros_a = jnp.zeros((n_a, Bp) + state_conv_a.shape[2:], F32)
    zeros_b = jnp.zeros((n_b, Bp) + state_conv_b.shape[2:], F32)
    (y_p, a_p, b_p, k_p, v_p), _ = _trunk(x_prompt, 0, mod, zeros_a, zeros_b, None, None, 0, p, bf16_weights)

    def heads(a):
        return a.reshape(a.shape[:3] + (H, dh2))
    return (y_p, y_s, a_p, a_s, b_p, b_s, heads(k_p), heads(v_p), heads(k_s), heads(v_s))
```

```python
import functools
import math

import numpy as np
import jax
import jax.numpy as jnp
from jax import lax
from jax.experimental import pallas as pl
from jax.experimental.pallas import tpu as pltpu

F32 = jnp.float32
BF16 = jnp.bfloat16

N_HEADS = 8
CHUNK = 64
FFN_RES = 0.5
NORM_EPS = 1e-6
SUBLN_EPS = 1e-5
LN_EPS = 1e-5
NEG_INF = -1e30
LOG2E = math.log2(math.e)
N_SUB = 3
N_MIXERS = 3

SEQ_PAD = 16
V7X_VMEM_LIMIT_BYTES = 56 * 1024 * 1024
V7X_VMEM_LIMIT_FFN_BYTES = 60 * 1024 * 1024
SUBLANES = 8
LANES = 128
ROW_BLK = 16
ROW_UNROLL = 4

FFN_ROWS = 1024
FFN_FF_TILE = 512
FFN_DOWN_TILE = 512
MM_ROWS = 512
QKV_ROWS = 1024
PRE_N_TILE = 1024
FRONT_ROWS = 1024
FRONT_COLS_A = 512
FRONT_COLS_B = 256
FRONT_SUB_TILES = 1
CONV_SUB_ROWS = 64
ATTN_Q_TILE = 1024
ATTN_K_TILE = 512
ATTN_HEADS_PER_STEP = 4


def _cparams(sem, vmem_limit_bytes=V7X_VMEM_LIMIT_BYTES):
    return pltpu.CompilerParams(dimension_semantics=sem, vmem_limit_bytes=vmem_limit_bytes)


def _row_blocks(G, R, rows):
    if R >= rows:
        assert R % rows == 0
        return 1, rows
    gb = min(G, rows // R)
    assert G % gb == 0 and R % ROW_BLK == 0
    return gb, R


def _silu(x):
    return x * jax.nn.sigmoid(x)


def _dot(a, b):
    return jnp.dot(a, b, preferred_element_type=F32)


def _mod_row(ref, seq0, b):
    return ref[pl.ds(seq0 + b, 1), :]


def _row_loop(n_rows, fn, static=False):
    nblk = n_rows // ROW_BLK
    if static:
        for i in range(nblk):
            fn(i * ROW_BLK)
        return

    def step(i, carry):
        fn(pl.multiple_of(i * ROW_BLK, ROW_BLK))
        return carry
    lax.fori_loop(0, nblk, step, 0, unroll=min(ROW_UNROLL, nblk))


def _set_vec(vec_ref, k, row):
    vec_ref[k] = jnp.broadcast_to(row, vec_ref.shape[1:])


def _prenorm_rows(x_ref, h_ref, vec_ref, gpre_ref, sh_ref, sc_ref, seq0, zero_ref=None):
    gb, rb, _ = x_ref.shape
    for b in range(gb):
        _set_vec(vec_ref, 0, gpre_ref[...] * (1.0 + _mod_row(sc_ref, seq0, b)))
        _set_vec(vec_ref, 1, _mod_row(sh_ref, seq0, b))

        def blk(r0, b=b):
            x = x_ref[b, r0:r0 + ROW_BLK, :]
            r = lax.rsqrt(jnp.mean(x * x, axis=-1, keepdims=True) + NORM_EPS)
            h_ref[b * rb + r0:b * rb + r0 + ROW_BLK, :] = ((x * r) * vec_ref[0] + vec_ref[1]).astype(BF16)
            if zero_ref is not None:
                zero_ref[b, r0:r0 + ROW_BLK, :] = jnp.zeros((ROW_BLK, zero_ref.shape[2]), F32)
        _row_loop(rb, blk, static=True)


def _postnorm_rows(o_ref, x_ref, vec_ref, gpost_ref, gt_ref, seq0, rw):
    gb, rb, _ = x_ref.shape
    for b in range(gb):
        _set_vec(vec_ref, 0, (rw * _mod_row(gt_ref, seq0, b)) * gpost_ref[...])

        def blk(r0, b=b):
            o = o_ref[b, r0:r0 + ROW_BLK, :]
            r = lax.rsqrt(jnp.mean(o * o, axis=-1, keepdims=True) + NORM_EPS)
            o_ref[b, r0:r0 + ROW_BLK, :] = x_ref[b, r0:r0 + ROW_BLK, :] + (o * r) * vec_ref[0]
        _row_loop(rb, blk, static=True)


def _adaln_body(c_ref, w_ref, b_ref, o_ref):
    a = _silu(c_ref[...]).astype(BF16)
    o_ref[...] = _dot(a, w_ref[...].astype(BF16)) + b_ref[...]


def _adaln_mod(c_all, w_mod, b_mod):
    L, D, N = w_mod.shape
    nm = N // D
    P = c_all.shape[0]
    return pl.pallas_call(
        _adaln_body,
        grid=(L, nm),
        in_specs=[pl.BlockSpec((P, D), lambda l, m: (0, 0)),
                  pl.BlockSpec((None, D, D), lambda l, m: (l, 0, m)),
                  pl.BlockSpec((None, 1, D), lambda l, m: (l * nm + m, 0, 0))],
        out_specs=pl.BlockSpec((None, P, D), lambda l, m: (l * nm + m, 0, 0)),
        out_shape=jax.ShapeDtypeStruct((L * nm, P, D), F32),
        compiler_params=_cparams(("arbitrary", "arbitrary")),
        name="adaln_mod",
    )(c_all, w_mod, b_mod.reshape(L * nm, 1, D))


def _mod_specs(layer, sub, kinds, D):
    specs = []
    for k in kinds:
        idx = layer * 3 * N_SUB + 3 * sub + k
        specs.append(pl.BlockSpec((None, SEQ_PAD, D), lambda *_, idx=idx: (idx, 0, 0)))
    return specs


def _ffn_body(x_ref, sh_ref, sc_ref, gt_ref, gpre_ref, gpost_ref, wg_ref, wu_ref, wd_ref, *rest,
              row0, n_chunks, emit):
    if emit:
        o_ref, wgo_ref, wuo_ref, wdo_ref, h_ref, vec_ref = rest
    else:
        o_ref, h_ref, vec_ref = rest
    j = pl.program_id(2)
    gb, rb, _ = x_ref.shape
    seq0 = row0 + pl.program_id(0) * gb

    @pl.when(j == 0)
    def _():
        _prenorm_rows(x_ref, h_ref, vec_ref, gpre_ref, sh_ref, sc_ref, seq0, zero_ref=o_ref)

    if emit:
        wg, wu, wd = wg_ref[...].astype(BF16), wu_ref[...].astype(BF16), wd_ref[...].astype(BF16)
        wgo_ref[...] = wg
        wuo_ref[...] = wu
        wdo_ref[...] = wd
    else:
        wg, wu, wd = wg_ref[...], wu_ref[...], wd_ref

    h = h_ref[...]
    act = (_silu(_dot(h, wg)) * _dot(h, wu)).astype(BF16)
    dn = min(FFN_DOWN_TILE, o_ref.shape[2])
    for n0 in range(0, o_ref.shape[2], dn):
        y = _dot(act, wd[:, n0:n0 + dn])
        for b in range(gb):
            o_ref[b, :, n0:n0 + dn] += y[b * rb:(b + 1) * rb, :]

    @pl.when(j == n_chunks - 1)
    def _():
        _postnorm_rows(o_ref, x_ref, vec_ref, gpost_ref, gt_ref, seq0, FFN_RES)


def _ffn(x, mod, row0, layer, sub, g_pre, g_post, weights):
    G, R, D = x.shape
    emit = weights[0] == "f32"
    tf = FFN_FF_TILE
    gb, rb = _row_blocks(G, R, FFN_ROWS)
    grid_rows = (G // gb, R // rb)
    if emit:
        _, w_gu, w_down, ffn_idx = weights
        dff = w_down.shape[2]
        n_chunks = dff // tf
        assert grid_rows == (1, 1)
        wspecs = [pl.BlockSpec((None, None, D, tf), lambda g, t, j: (layer, ffn_idx, 0, j)),
                  pl.BlockSpec((None, None, D, tf), lambda g, t, j: (layer, ffn_idx, 0, j + n_chunks)),
                  pl.BlockSpec((None, None, tf, D), lambda g, t, j: (layer, ffn_idx, j, 0))]
        wargs = [w_gu, w_gu, w_down]
    else:
        _, wg, wu, wd = weights
        dff = wd.shape[0]
        n_chunks = dff // tf
        wargs = [wg, wu, wd]
    copy_specs = [pl.BlockSpec((D, tf), lambda g, t, j: (0, j)),
                  pl.BlockSpec((D, tf), lambda g, t, j: (0, j)),
                  pl.BlockSpec((tf, D), lambda g, t, j: (j, 0))]
    if not emit:
        wspecs = copy_specs
    gidx = layer * N_SUB + sub
    body = functools.partial(_ffn_body, row0=row0, n_chunks=n_chunks, emit=emit)
    xspec = pl.BlockSpec((gb, rb, D), lambda g, t, j: (g, t, 0))
    out_specs, out_shape = [xspec], [jax.ShapeDtypeStruct((G, R, D), F32)]
    if emit:
        out_specs += copy_specs
        out_shape += [jax.ShapeDtypeStruct((D, dff), BF16), jax.ShapeDtypeStruct((D, dff), BF16),
                      jax.ShapeDtypeStruct((dff, D), BF16)]
    outs = pl.pallas_call(
        body,
        grid=grid_rows + (n_chunks,),
        in_specs=[xspec] + _mod_specs(layer, sub, (0, 1, 2), D) + [
            pl.BlockSpec((None, 1, D), lambda g, t, j: (gidx, 0, 0)),
            pl.BlockSpec((None, 1, D), lambda g, t, j: (gidx, 0, 0))] + wspecs,
        out_specs=out_specs,
        out_shape=out_shape,
        scratch_shapes=[pltpu.VMEM((gb * rb, D), BF16), pltpu.VMEM((2, ROW_BLK, D), F32)],
        compiler_params=_cparams(("arbitrary", "arbitrary", "arbitrary"), V7X_VMEM_LIMIT_FFN_BYTES),
        name="ffn_cast" if emit else "ffn",
    )(x, mod, mod, mod, g_pre, g_post, *wargs)
    return (outs[0], tuple(outs[1:])) if emit else (outs[0], None)


def _qkv_body(x_ref, sh_ref, sc_ref, gpre_ref, w_ref, qkv_ref, k_ref, v_ref, h_ref, vec_ref, *, row0, nb, q_scale):
    n = pl.program_id(2)
    gb, rb, _ = x_ref.shape
    seq0 = row0 + pl.program_id(0) * gb

    @pl.when(n == 0)
    def _():
        _prenorm_rows(x_ref, h_ref, vec_ref, gpre_ref, sh_ref, sc_ref, seq0)

    z = _dot(h_ref[...], w_ref[...])
    zs = z * jnp.where(n < nb, q_scale, 1.0)
    for b in range(gb):
        qkv_ref[b] = zs[b * rb:(b + 1) * rb, :].astype(BF16)

    @pl.when((n >= nb) & (n < 2 * nb))
    def _():
        for b in range(gb):
            k_ref[b] = z[b * rb:(b + 1) * rb, :]

    @pl.when(n >= 2 * nb)
    def _():
        for b in range(gb):
            v_ref[b] = z[b * rb:(b + 1) * rb, :]


def _qkv_proj(x, mod, row0, layer, g_pre, w, widx, q_scale):
    G, R, D = x.shape
    tn = PRE_N_TILE
    nb = D // tn
    gb, rb = _row_blocks(G, R, QKV_ROWS)
    gidx = layer * N_SUB + 1
    body = functools.partial(_qkv_body, row0=row0, nb=nb, q_scale=q_scale)
    kspec = pl.BlockSpec((gb, rb, tn), lambda g, t, n: (g, t, jnp.clip(n - nb, 0, nb - 1)))
    vspec = pl.BlockSpec((gb, rb, tn), lambda g, t, n: (g, t, jnp.clip(n - 2 * nb, 0, nb - 1)))
    return pl.pallas_call(
        body,
        grid=(G // gb, R // rb, 3 * nb),
        in_specs=[pl.BlockSpec((gb, rb, D), lambda g, t, n: (g, t, 0))] + _mod_specs(layer, 1, (0, 1), D) + [
            pl.BlockSpec((None, 1, D), lambda g, t, n: (gidx, 0, 0)),
            pl.BlockSpec((None, D, tn), lambda g, t, n: (widx, 0, n))],
        out_specs=[pl.BlockSpec((gb, rb, tn), lambda g, t, n: (g, t, n)), kspec, vspec],
        out_shape=[jax.ShapeDtypeStruct((G, R, 3 * D), BF16), jax.ShapeDtypeStruct((G, R, D), F32),
                   jax.ShapeDtypeStruct((G, R, D), F32)],
        scratch_shapes=[pltpu.VMEM((gb * rb, D), BF16), pltpu.VMEM((2, ROW_BLK, D), F32)],
        compiler_params=_cparams(("arbitrary", "arbitrary", "arbitrary")),
        name="qkv_proj",
    )(x, mod, mod, g_pre, w)


def _mm_post_body(y_ref, w_ref, x_ref, gt_ref, gpost_ref, *rest, row0, has_bias, has_ln):
    rest = list(rest)
    b_ref = rest.pop(0) if has_bias else None
    lng_ref, lnb_ref = (rest.pop(0), rest.pop(0)) if has_ln else (None, None)
    o_ref = rest.pop(0)
    a_ref = rest.pop(0) if has_ln else None
    vec_ref = rest.pop(0)
    gb, rb, _ = x_ref.shape
    seq0 = row0 + pl.program_id(0) * gb

    if has_ln:
        _set_vec(vec_ref, 0, lng_ref[...])
        _set_vec(vec_ref, 1, lnb_ref[...])
        for b in range(gb):
            def blk(r0, b=b):
                y = y_ref[b, r0:r0 + ROW_BLK, :]
                yc = y - jnp.mean(y, axis=-1, keepdims=True)
                n = yc * lax.rsqrt(jnp.mean(yc * yc, axis=-1, keepdims=True) + LN_EPS) * vec_ref[0] + vec_ref[1]
                a_ref[b * rb + r0:b * rb + r0 + ROW_BLK, :] = _silu(n).astype(BF16)
            _row_loop(rb, blk, static=True)
        a = a_ref[...]
    else:
        a = y_ref[...].reshape(gb * rb, y_ref.shape[2])

    o = _dot(a, w_ref[...])
    if has_bias:
        o = o + b_ref[...]
    for b in range(gb):
        o_ref[b] = o[b * rb:(b + 1) * rb, :]
    _postnorm_rows(o_ref, x_ref, vec_ref, gpost_ref, gt_ref, seq0, 1.0)


def _mm_post(y, x, mod, row0, layer, g_post, w, widx, bias=None, ln=None):
    G, R, D = x.shape
    Din = y.shape[2]
    gb, rb = _row_blocks(G, R, MM_ROWS)
    gidx = layer * N_SUB + 1
    has_bias, has_ln = bias is not None, ln is not None
    assert (Din == D) if has_ln else (y.dtype == BF16)
    body = functools.partial(_mm_post_body, row0=row0, has_bias=has_bias, has_ln=has_ln)
    xspec = pl.BlockSpec((gb, rb, D), lambda g, t: (g, t, 0))
    in_specs = [pl.BlockSpec((gb, rb, Din), lambda g, t: (g, t, 0)),
                pl.BlockSpec((None, Din, D), lambda g, t: (widx, 0, 0)),
                xspec] + _mod_specs(layer, 1, (2,), D) + [
        pl.BlockSpec((None, 1, D), lambda g, t: (gidx, 0, 0))]
    args = [y, w, x, mod, g_post]
    if has_bias:
        in_specs.append(pl.BlockSpec((None, 1, D), lambda g, t: (widx, 0, 0)))
        args.append(bias.reshape(bias.shape[0], 1, D))
    if has_ln:
        for p in ln:
            in_specs.append(pl.BlockSpec((None, 1, Din), lambda g, t: (widx, 0, 0)))
            args.append(p.reshape(p.shape[0], 1, Din))
    scratch = ([pltpu.VMEM((gb * rb, Din), BF16)] if has_ln else []) + [pltpu.VMEM((2, ROW_BLK, D), F32)]
    return pl.pallas_call(
        body,
        grid=(G // gb, R // rb),
        in_specs=in_specs,
        out_specs=xspec,
        out_shape=jax.ShapeDtypeStruct((G, R, D), F32),
        scratch_shapes=scratch,
        compiler_params=_cparams(("arbitrary", "arbitrary")),
        name="mm_post",
    )(*args)


def _conv_stage(ext_ref, carry_ref, hist_ref, nh_ref, slot, b, c, t, s, n_sub, new_rows, W):
    sr = new_rows.shape[0]
    pad = ext_ref.shape[2] - sr
    lo = pad - (W - 1)
    if s == 0:
        @pl.when(t == 0)
        def _():
            ext_ref[slot, b, lo:pad, :] = hist_ref[b]

        @pl.when(t > 0)
        def _():
            ext_ref[slot, b, lo:pad, :] = carry_ref[c, b, lo:pad, :]
    else:
        ext_ref[slot, b, lo:pad, :] = ext_ref[1 - slot, b, lo + sr:pad + sr, :]
    ext_ref[slot, b, pad:pad + sr, :] = new_rows
    if s == n_sub - 1:
        tail = ext_ref[slot, b, lo + sr:pad + sr, :]
        carry_ref[c, b, lo:pad, :] = tail
        tc = ext_ref.shape[3]
        nh_ref[b, :, pl.ds(pl.multiple_of(c * tc, tc), tc)] = tail
    return lo


def _front_a_body(x_ref, sh_ref, sc_ref, gpre_ref, wb_ref, wc_ref, wx_ref, hist_ref, wconv_ref,
                  y_ref, nh_ref, h_ref, vec_ref, ext_ref, carry_ref, *, row0, n_sub):
    t, c = pl.program_id(1), pl.program_id(2)
    gb, rb, _ = x_ref.shape
    W = wconv_ref.shape[0]
    seq0 = row0 + pl.program_id(0) * gb

    @pl.when(c == 0)
    def _():
        _prenorm_rows(x_ref, h_ref, vec_ref, gpre_ref, sh_ref, sc_ref, seq0)

    sr = rb // n_sub
    rs = min(CONV_SUB_ROWS, sr)
    for b in range(gb):
        for s in range(n_sub):
            slot = s % 2
            h = h_ref[b * rb + s * sr:b * rb + (s + 1) * sr, :]
            zb, zc, zx = _dot(h, wb_ref[...]), _dot(h, wc_ref[...]), _dot(h, wx_ref[...])
            lo = _conv_stage(ext_ref, carry_ref, hist_ref, nh_ref, slot, b, c, t, s, n_sub, zc * zx, W)
            for r0 in range(0, sr, rs):
                acc = wconv_ref[0:1, :] * ext_ref[slot, b, lo + r0:lo + r0 + rs, :]
                for j in range(1, W):
                    acc = acc + wconv_ref[j:j + 1, :] * ext_ref[slot, b, lo + j + r0:lo + j + r0 + rs, :]
                y_ref[b, s * sr + r0:s * sr + r0 + rs, :] = (zb[r0:r0 + rs] * acc).astype(y_ref.dtype)


def _front_b_body(x_ref, sh_ref, sc_ref, gpre_ref, wa_ref, wg_ref, ba_ref, bg_ref, hist_ref, wconv_ref, bdw_ref,
                  y_ref, nh_ref, h_ref, vec_ref, ext_ref, carry_ref, ph_ref, *, row0, n_sub):
    t, c = pl.program_id(1), pl.program_id(2)
    gb, rb, _ = x_ref.shape
    W = wconv_ref.shape[0]
    seq0 = row0 + pl.program_id(0) * gb

    @pl.when(c == 0)
    def _():
        _prenorm_rows(x_ref, h_ref, vec_ref, gpre_ref, sh_ref, sc_ref, seq0)

    sr = rb // n_sub
    rs = min(CONV_SUB_ROWS, sr)
    n_shift = ph_ref.shape[2]
    for b in range(gb):
        for s in range(n_sub):
            slot = s % 2
            h = h_ref[b * rb + s * sr:b * rb + (s + 1) * sr, :]
            u = (_dot(h, wa_ref[...]) + ba_ref[...]) * jax.nn.sigmoid(_dot(h, wg_ref[...]) + bg_ref[...])
            lo = _conv_stage(ext_ref, carry_ref, hist_ref, nh_ref, slot, b, c, t, s, n_sub, u, W)
            for p in range(1, SUBLANES):
                ph_ref[slot, p - 1] = ext_ref[slot, b, p:p + n_shift, :]
            for r0 in range(0, sr, rs):
                acc = None
                for j in range(W):
                    a, p = divmod(lo + j, SUBLANES)
                    start = r0 + a * SUBLANES
                    win = (ext_ref[slot, b, start:start + rs, :] if p == 0
                           else ph_ref[slot, p - 1, start:start + rs, :])
                    term = wconv_ref[j:j + 1, :] * win
                    acc = term if acc is None else acc + term
                y_ref[b, s * sr + r0:s * sr + r0 + rs, :] = acc + bdw_ref[...]


def _mixer_front(kind, x, mod, row0, layer, g_pre, w, widx, hist, wconv, bias=None, bdw=None):
    G, R, D = x.shape
    nsplit = 3 if kind == "a" else 2
    W = wconv.shape[1]
    gb, rb = _row_blocks(G, R, FRONT_ROWS)
    n_sub = FRONT_SUB_TILES if rb % (FRONT_SUB_TILES * CONV_SUB_ROWS) == 0 else 1
    sr = rb // n_sub
    assert sr >= W - 1 and sr % min(CONV_SUB_ROWS, sr) == 0
    tc = FRONT_COLS_A if kind == "a" else FRONT_COLS_B
    nc = D // tc
    pad = -(-(W - 1) // SUBLANES) * SUBLANES
    gidx = layer * N_SUB + 1
    wspecs = [pl.BlockSpec((None, D, tc), lambda g, t, c, s=s: (widx, 0, s * nc + c)) for s in range(nsplit)]
    hist_spec = pl.BlockSpec((gb, W - 1, tc), lambda g, t, c: (g, 0, c))
    in_specs = [pl.BlockSpec((gb, rb, D), lambda g, t, c: (g, t, 0))] + _mod_specs(layer, 1, (0, 1), D) + [
        pl.BlockSpec((None, 1, D), lambda g, t, c: (gidx, 0, 0))] + wspecs
    args = [x, mod, mod, g_pre] + [w] * nsplit
    scratch = [pltpu.VMEM((gb * rb, D), BF16), pltpu.VMEM((2, ROW_BLK, D), F32),
               pltpu.VMEM((2, gb, pad + sr, tc), F32), pltpu.VMEM((nc, gb, pad, tc), F32)]
    if kind == "a":
        body, out_dtype = _front_a_body, BF16
    else:
        body, out_dtype = _front_b_body, F32
        in_specs += [pl.BlockSpec((None, 1, tc), lambda g, t, c, s=s: (widx, 0, s * nc + c)) for s in range(nsplit)]
        args += [bias.reshape(bias.shape[0], 1, nsplit * D)] * nsplit
        scratch.append(pltpu.VMEM((2, SUBLANES - 1, pad + sr - SUBLANES, tc), F32))
    in_specs += [hist_spec, pl.BlockSpec((None, W, tc), lambda g, t, c: (widx, 0, c))]
    args += [hist, wconv]
    if kind == "b":
        in_specs.append(pl.BlockSpec((None, 1, tc), lambda g, t, c: (widx, 0, c)))
        args.append(bdw.reshape(bdw.shape[0], 1, D))
    return pl.pallas_call(
        functools.partial(body, row0=row0, n_sub=n_sub),
        grid=(G // gb, R // rb, nc),
        in_specs=in_specs,
        out_specs=[pl.BlockSpec((gb, rb, tc), lambda g, t, c: (g, t, c)),
                   pl.BlockSpec((gb, W - 1, D), lambda g, t, c: (g, 0, 0))],
        out_shape=[jax.ShapeDtypeStruct((G, R, D), out_dtype),
                   jax.ShapeDtypeStruct((G, W - 1, D), F32)],
        scratch_shapes=scratch,
        compiler_params=_cparams(("arbitrary", "arbitrary", "arbitrary")),
        name="front_" + kind,
    )(*args)


def _last_visible_tile(q_pos, tk):
    return ((q_pos // CHUNK) * CHUNK + CHUNK - 1) // tk


def _lambda(lam_ref, lam_init):
    lp = lam_ref[...]
    return (jnp.exp(jnp.sum(lp[0:1] * lp[1:2], axis=-1, keepdims=True))
            - jnp.exp(jnp.sum(lp[2:3] * lp[3:4], axis=-1, keepdims=True)) + lam_init)


def _diff_out(o1, o2, lam, subg, lam_init):
    o = o1 - lam * o2
    r = o * lax.rsqrt(jnp.mean(o * o, axis=0, keepdims=True) + SUBLN_EPS) * subg
    return (r * (1.0 - lam_init)).T


def _attn_body(qi_ref, ki_ref, last_ref, slope_ref, lam_ref, subg_ref, q_ref, k_ref, v_ref, o_ref,
               m_ref, l_ref, acc_ref, ramp_ref, *, pos0, n_keys, lam_init):
    hp, s = pl.program_id(1), pl.program_id(2)
    qi, ki = qi_ref[s], ki_ref[s]
    tq = q_ref.shape[0]
    tk = k_ref.shape[0]
    dh2 = subg_ref.shape[0]
    dh = dh2 // 2
    n_heads = q_ref.shape[1] // dh2
    shift = int(math.log2(CHUNK))
    slopes = [slope_ref[hp * n_heads + hh] for hh in range(n_heads)]

    @pl.when(s == 0)
    def _():
        d = (lax.broadcasted_iota(jnp.int32, (tk, tq), 1) - lax.broadcasted_iota(jnp.int32, (tk, tq), 0)).astype(F32)
        for hh in range(n_heads):
            ramp_ref[hh] = -slopes[hh] * d

    @pl.when(ki == 0)
    def _():
        m_ref[...] = jnp.full(m_ref.shape, NEG_INF, F32)
        l_ref[...] = jnp.zeros(l_ref.shape, F32)
        acc_ref[...] = jnp.zeros(acc_ref.shape, F32)

    q_lo = pos0 + qi * tq
    k_lo = ki * tk

    def raw_scores(hh, c):
        col = hh * dh2 + c * dh
        return lax.dot_general(k_ref[:, col:col + dh], q_ref[:, col:col + dh],
                               (((1,), (1,)), ((), ())), preferred_element_type=F32)

    def online_update(hh, c, t, cst):
        i = 2 * hh + c
        m_old = m_ref[i]
        m_new = jnp.maximum(m_old, jnp.max(t, axis=0, keepdims=True) + cst)
        alpha = jnp.exp2(m_old - m_new)
        p = jnp.exp2(t + (cst - m_new))
        l_ref[i] = alpha * l_ref[i] + jnp.sum(p, axis=0, keepdims=True)
        pv = lax.dot_general(v_ref[:, hh * dh2:(hh + 1) * dh2], p.astype(BF16), (((0,), (0,)), ((), ())),
                             preferred_element_type=F32)
        acc_ref[i] = alpha * acc_ref[i] + pv
        m_ref[i] = m_new

    all_past = k_lo + tk - 1 < q_lo

    @pl.when(all_past)
    def _():
        for hh in range(n_heads):
            cst = -slopes[hh] * (q_lo - k_lo).astype(F32)
            for c in range(2):
                online_update(hh, c, raw_scores(hh, c) + ramp_ref[hh], cst)

    @pl.when(jnp.logical_not(all_past))
    def _():
        qpos = q_lo + lax.broadcasted_iota(jnp.int32, (1, tq), 1)
        kpos = k_lo + lax.broadcasted_iota(jnp.int32, (tk, 1), 0)
        visible = (jnp.right_shift(kpos, shift) <= jnp.right_shift(qpos, shift)) & (kpos < n_keys)
        dist = jnp.abs(qpos - kpos).astype(F32)
        for hh in range(n_heads):
            bias = -slopes[hh] * dist
            for c in range(2):
                online_update(hh, c, jnp.where(visible, raw_scores(hh, c) + bias, NEG_INF), 0.0)

    @pl.when(last_ref[s] == 1)
    def _():
        lam = _lambda(lam_ref, lam_init)
        for hh in range(n_heads):
            o_ref[:, hh * dh2:(hh + 1) * dh2] = _diff_out(
                acc_ref[2 * hh] / l_ref[2 * hh], acc_ref[2 * hh + 1] / l_ref[2 * hh + 1], lam,
                subg_ref[...], lam_init).astype(o_ref.dtype)


def _diff_attn(q_arr, q_off, k_arr, k_off, v_arr, v_off, n_keys, pos0, lam_params, subln_g, idx, lam_init,
               slopes):
    B, Tq, _ = q_arr.shape
    Tk = k_arr.shape[1]
    dh2 = subln_g.shape[1]
    H = N_HEADS
    tq = min(ATTN_Q_TILE, Tq)
    tk = Tk if Tk <= 2 * ATTN_K_TILE + LANES else ATTN_K_TILE
    assert Tq % tq == 0 and Tk % tk == 0 and pos0 + Tq <= n_keys <= Tk
    assert 1 << int(math.log2(CHUNK)) == CHUNK
    hps = ATTN_HEADS_PER_STEP if H % ATTN_HEADS_PER_STEP == 0 else 1
    assert q_off % hps == 0 and k_off % hps == 0 and v_off % hps == 0
    nk = Tk // tk
    qis, kis, lasts = [], [], []
    for qi in range(Tq // tq):
        last = min(_last_visible_tile(pos0 + (qi + 1) * tq - 1, tk), nk - 1)
        for ki in range(last + 1):
            qis.append(qi)
            kis.append(ki)
            lasts.append(int(ki == last))
    sched = [jnp.asarray(np.asarray(a, np.int32)) for a in (qis, kis, lasts)]

    body = functools.partial(_attn_body, pos0=pos0, n_keys=n_keys, lam_init=lam_init)
    grid_spec = pltpu.PrefetchScalarGridSpec(
        num_scalar_prefetch=3,
        grid=(B, H // hps, len(qis)),
        in_specs=[pl.BlockSpec(memory_space=pltpu.SMEM),
                  pl.BlockSpec((None, 4, dh2 // 2), lambda b, h, s, qi, ki, la: (idx, 0, 0)),
                  pl.BlockSpec((None, dh2, 1), lambda b, h, s, qi, ki, la: (idx, 0, 0)),
                  pl.BlockSpec((None, tq, hps * dh2), lambda b, h, s, qi, ki, la: (b, qi[s], q_off // hps + h)),
                  pl.BlockSpec((None, tk, hps * dh2), lambda b, h, s, qi, ki, la: (b, ki[s], k_off // hps + h)),
                  pl.BlockSpec((None, tk, hps * dh2), lambda b, h, s, qi, ki, la: (b, ki[s], v_off // hps + h))],
        out_specs=pl.BlockSpec((None, tq, hps * dh2), lambda b, h, s, qi, ki, la: (b, qi[s], h)),
        scratch_shapes=[pltpu.VMEM((2 * hps, 1, tq), F32), pltpu.VMEM((2 * hps, 1, tq), F32),
                        pltpu.VMEM((2 * hps, dh2, tq), F32), pltpu.VMEM((hps, tk, tq), F32)])
    return pl.pallas_call(
        body,
        grid_spec=grid_spec,
        out_shape=jax.ShapeDtypeStruct((B, Tq, H * dh2), BF16),
        compiler_params=_cparams(("arbitrary", "arbitrary", "arbitrary")),
        name="diff_attn",
    )(*sched, slopes * LOG2E, lam_params, subln_g.reshape(subln_g.shape[0], dh2, 1), q_arr, k_arr, v_arr)


def _cached_attn_body(slope_ref, lam_ref, subg_ref, qkv_ref, ck1_ref, ck2_ref, cv1_ref, cv2_ref, o_ref,
                      *, past, lam_init):
    R = qkv_ref.shape[0]
    dh = ck1_ref.shape[1]
    dh2 = 2 * dh
    H = ck1_ref.shape[0] // past
    D = H * dh2
    shift = int(math.log2(CHUNK))
    trans_b = (((1,), (1,)), ((), ()))

    d_cache = (past + lax.broadcasted_iota(jnp.int32, (R, past), 0)
               - lax.broadcasted_iota(jnp.int32, (R, past), 1)).astype(F32)
    qpos = past + lax.broadcasted_iota(jnp.int32, (R, 1), 0)
    kpos = past + lax.broadcasted_iota(jnp.int32, (1, R), 1)
    vis_new = jnp.right_shift(kpos, shift) <= jnp.right_shift(qpos, shift)
    d_new = jnp.abs(qpos - kpos).astype(F32)
    lam = _lambda(lam_ref, lam_init)

    for h in range(H):
        slope = slope_ref[h]
        head_rows = pl.ds(h, past, stride=H)
        kc = (ck1_ref[head_rows, :].astype(BF16), ck2_ref[head_rows, :].astype(BF16))
        vc = jnp.concatenate([cv1_ref[head_rows, :].astype(BF16), cv2_ref[head_rows, :].astype(BF16)], axis=-1)
        q = qkv_ref[:, h * dh2:(h + 1) * dh2]
        kn = qkv_ref[:, D + h * dh2:D + (h + 1) * dh2]
        vn = qkv_ref[:, 2 * D + h * dh2:2 * D + (h + 1) * dh2]
        outs = []
        for c in range(2):
            qc = q[:, c * dh:(c + 1) * dh]
            t1 = lax.dot_general(qc, kc[c], trans_b, preferred_element_type=F32) - slope * d_cache
            t2 = lax.dot_general(qc, kn[:, c * dh:(c + 1) * dh], trans_b, preferred_element_type=F32) - slope * d_new
            t2 = jnp.where(vis_new, t2, NEG_INF)
            m = jnp.maximum(jnp.max(t1, axis=-1, keepdims=True), jnp.max(t2, axis=-1, keepdims=True))
            p1 = jnp.exp2(t1 - m)
            p2 = jnp.exp2(t2 - m)
            l = jnp.sum(p1, axis=-1, keepdims=True) + jnp.sum(p2, axis=-1, keepdims=True)
            outs.append((_dot(p1.astype(BF16), vc) + _dot(p2.astype(BF16), vn)) / l)
        o = outs[0] - lam * outs[1]
        r = o * lax.rsqrt(jnp.mean(o * o, axis=-1, keepdims=True) + SUBLN_EPS) * subg_ref[...]
        o_ref[:, h * dh2:(h + 1) * dh2] = (r * (1.0 - lam_init)).astype(o_ref.dtype)


def _cached_attn(qkv, cache_k, cache_v, idx, lam_params, subln_g, lam_init, slopes):
    B, R, _ = qkv.shape
    n_c, _, past, H, dh2 = cache_k.shape
    D = H * dh2
    body = functools.partial(_cached_attn_body, past=past, lam_init=lam_init)
    cspecs = [pl.BlockSpec((None, None, past * H, dh2 // 2), lambda b, half=half: (idx, b, 0, half))
              for half in (0, 1)]
    return pl.pallas_call(
        body,
        grid=(B,),
        in_specs=[pl.BlockSpec(memory_space=pltpu.SMEM),
                  pl.BlockSpec((None, 4, dh2 // 2), lambda b: (idx, 0, 0)),
                  pl.BlockSpec((None, 1, dh2), lambda b: (idx, 0, 0)),
                  pl.BlockSpec((None, R, 3 * D), lambda b: (b, 0, 0)),
                  ] + cspecs + cspecs,
        out_specs=pl.BlockSpec((None, R, D), lambda b: (b, 0, 0)),
        out_shape=jax.ShapeDtypeStruct((B, R, D), BF16),
        compiler_params=_cparams(("arbitrary",)),
        name="cached_attn",
    )(slopes * LOG2E, lam_params, subln_g.reshape(subln_g.shape[0], 1, dh2), qkv,
      *([cache_k.reshape(n_c, B, past * H, dh2)] * 2 + [cache_v.reshape(n_c, B, past * H, dh2)] * 2))


def _trunk(x, row0, mod, hist_a, hist_b, hist_k, hist_v, pos0, p, ffn_weights):
    G, R, D = x.shape
    depth = p["depth"]
    slopes = jnp.asarray([2.0 ** (-8.0 * (h + 1) / N_HEADS) for h in range(N_HEADS)], dtype=F32)
    new_a, new_b, new_k, new_v, copies = [], [], [], [], {}
    for i in range(depth):
        x, copies[i, 0] = _ffn(x, mod, row0, i, 0, p["g_pre"], p["g_post"], ffn_weights[i, 0])
        kind, idx = i % N_MIXERS, i // N_MIXERS
        if kind == 0:
            y, st = _mixer_front("a", x, mod, row0, i, p["g_pre"], p["sc_w_in"], idx, hist_a[idx], p["sc_w_conv"])
            new_a.append(st)
            x = _mm_post(y, x, mod, row0, i, p["g_post"], p["sc_w_out"], idx)
        elif kind == 1:
            y, st = _mixer_front("b", x, mod, row0, i, p["g_pre"], p["cm_w_pw1"], idx, hist_b[idx], p["cm_w_dw"],
                                 bias=p["cm_b_pw1"], bdw=p["cm_b_dw"])
            new_b.append(st)
            x = _mm_post(y, x, mod, row0, i, p["g_post"], p["cm_w_pw2"], idx, bias=p["cm_b_pw2"],
                         ln=(p["cm_ln_g"], p["cm_ln_b"]))
        else:
            lam_init = 0.8 - 0.6 * math.exp(-0.3 * i)
            dh2 = p["attn_subln_g"].shape[1]
            qkv, k_rows, v_rows = _qkv_proj(x, mod, row0, i, p["g_pre"], p["attn_w_qkv"], idx,
                                            (dh2 // 2) ** -0.5 * LOG2E)
            new_k.append(k_rows)
            new_v.append(v_rows)
            nblk = D // dh2
            if hist_k is None:
                o = _diff_attn(qkv, 0, qkv, nblk, qkv, 2 * nblk, R, pos0, p["attn_lambda"], p["attn_subln_g"],
                               idx, lam_init, slopes)
            else:
                assert pos0 == hist_k.shape[2]
                o = _cached_attn(qkv, hist_k, hist_v, idx, p["attn_lambda"], p["attn_subln_g"], lam_init, slopes)
            x = _mm_post(o, x, mod, row0, i, p["g_post"], p["attn_w_o"], idx)
        x, copies[i, 1] = _ffn(x, mod, row0, i, 2, p["g_pre"], p["g_post"], ffn_weights[i, 1])
    return (x, jnp.stack(new_a), jnp.stack(new_b), jnp.stack(new_k), jnp.stack(new_v)), copies


def kernel(x_prompt, x_sample, state_conv_a, state_conv_b, cache_k, cache_v, c_prompt, c_sample, w_mod, b_mod, g_pre, g_post, ffn_w_gu, ffn_w_down, sc_w_in, sc_w_conv, sc_w_out, cm_w_pw1, cm_b_pw1, cm_w_dw, cm_b_dw, cm_ln_g, cm_ln_b, cm_w_pw2, cm_b_pw2, attn_w_qkv, attn_lambda, attn_subln_g, attn_w_o):
    Bp, T, D = x_prompt.shape
    Bs = x_sample.shape[0]
    depth = g_pre.shape[0]
    n_a, n_b = state_conv_a.shape[0], state_conv_b.shape[0]
    assert Bp + Bs <= SEQ_PAD
    H, dh2 = N_HEADS, attn_subln_g.shape[1]

    c_all = jnp.concatenate([c_prompt, c_sample, jnp.zeros((SEQ_PAD - Bp - Bs, D), F32)], axis=0)
    mod = _adaln_mod(c_all, w_mod, b_mod)

    p = dict(
        depth=depth,
        g_pre=g_pre.reshape(depth * N_SUB, 1, D), g_post=g_post.reshape(depth * N_SUB, 1, D),
        sc_w_in=sc_w_in.astype(BF16), sc_w_conv=sc_w_conv, sc_w_out=sc_w_out.astype(BF16),
        cm_w_pw1=cm_w_pw1.astype(BF16), cm_b_pw1=cm_b_pw1, cm_w_dw=cm_w_dw, cm_b_dw=cm_b_dw,
        cm_ln_g=cm_ln_g, cm_ln_b=cm_ln_b, cm_w_pw2=cm_w_pw2.astype(BF16), cm_b_pw2=cm_b_pw2,
        attn_w_qkv=attn_w_qkv.astype(BF16), attn_lambda=attn_lambda, attn_subln_g=attn_subln_g,
        attn_w_o=attn_w_o.astype(BF16))

    past = cache_k.shape[2]
    f32_weights = {(i, f): ("f32", ffn_w_gu, ffn_w_down, f) for i in range(depth) for f in range(2)}
    (y_s, a_s, b_s, k_s, v_s), copies = _trunk(x_sample, Bp, mod, state_conv_a, state_conv_b, cache_k, cache_v,
                                               past, p, f32_weights)
    bf16_weights = {key: ("bf16",) + copies[key] for key in copies}
    zeros_a = jnp.zeros((n_a, Bp) + state_conv_a.shape[2:], F32)
    zeros_b = jnp.zeros((n_b, Bp) + state_conv_b.shape[2:], F32)
    (y_p, a_p, b_p, k_p, v_p), _ = _trunk(x_prompt, 0, mod, zeros_a, zeros_b, None, None, 0, p, bf16_weights)

    def heads(a):
        return a.reshape(a.shape[:3] + (H, dh2))
    return (y_p, y_s, a_p, a_s, b_p, b_s, heads(k_p), heads(v_p), heads(k_s), heads(v_s))
```

```python
import functools
import math

import numpy as np
import jax
import jax.numpy as jnp
from jax import lax
from jax.experimental import pallas as pl
from jax.experimental.pallas import tpu as pltpu

F32 = jnp.float32
BF16 = jnp.bfloat16

N_HEADS = 8
CHUNK = 64
FFN_RES = 0.5
NORM_EPS = 1e-6
SUBLN_EPS = 1e-5
LN_EPS = 1e-5
NEG_INF = -1e30
LOG2E = math.log2(math.e)
N_SUB = 3
N_MIXERS = 3

SEQ_PAD = 16
V7X_VMEM_LIMIT_BYTES = 56 * 1024 * 1024
V7X_VMEM_LIMIT_FFN_BYTES = 60 * 1024 * 1024
SUBLANES = 8
LANES = 128
ROW_BLK = 16
ROW_UNROLL = 4

FFN_ROWS = 1024
FFN_FF_TILE = 512
FFN_DOWN_TILE = 512
MM_ROWS = 512
QKV_ROWS = 1024
PRE_N_TILE = 1024
FRONT_ROWS = 1024
FRONT_COLS_A = 512
FRONT_COLS_B = 256
FRONT_SUB_TILES = 1
CONV_SUB_ROWS = 64
ATTN_Q_TILE = 1024
ATTN_K_TILE = 512
ATTN_HEADS_PER_STEP = 4


def _cparams(sem, vmem_limit_bytes=V7X_VMEM_LIMIT_BYTES):
    return pltpu.CompilerParams(dimension_semantics=sem, vmem_limit_bytes=vmem_limit_bytes)


def _row_blocks(G, R, rows):
    if R >= rows:
        assert R % rows == 0
        return 1, rows
    gb = min(G, rows // R)
    assert G % gb == 0 and R % ROW_BLK == 0
    return gb, R


def _silu(x):
    return x * jax.nn.sigmoid(x)


def _dot(a, b):
    return jnp.dot(a, b, preferred_element_type=F32)


def _mod_row(ref, seq0, b):
    return ref[pl.ds(seq0 + b, 1), :]


def _row_loop(n_rows, fn, static=False):
    nblk = n_rows // ROW_BLK
    if static:
        for i in range(nblk):
            fn(i * ROW_BLK)
        return

    def step(i, carry):
        fn(pl.multiple_of(i * ROW_BLK, ROW_BLK))
        return carry
    lax.fori_loop(0, nblk, step, 0, unroll=min(ROW_UNROLL, nblk))


def _set_vec(vec_ref, k, row):
    vec_ref[k] = jnp.broadcast_to(row, vec_ref.shape[1:])


def _prenorm_rows(x_ref, h_ref, vec_ref, gpre_ref, sh_ref, sc_ref, seq0, zero_ref=None):
    gb, rb, _ = x_ref.shape
    for b in range(gb):
        _set_vec(vec_ref, 0, gpre_ref[...] * (1.0 + _mod_row(sc_ref, seq0, b)))
        _set_vec(vec_ref, 1, _mod_row(sh_ref, seq0, b))

        def blk(r0, b=b):
            x = x_ref[b, r0:r0 + ROW_BLK, :]
            r = lax.rsqrt(jnp.mean(x * x, axis=-1, keepdims=True) + NORM_EPS)
            h_ref[b * rb + r0:b * rb + r0 + ROW_BLK, :] = ((x * r) * vec_ref[0] + vec_ref[1]).astype(BF16)
            if zero_ref is not None:
                zero_ref[b, r0:r0 + ROW_BLK, :] = jnp.zeros((ROW_BLK, zero_ref.shape[2]), F32)
        _row_loop(rb, blk, static=True)


def _postnorm_rows(o_ref, x_ref, vec_ref, gpost_ref, gt_ref, seq0, rw):
    gb, rb, _ = x_ref.shape
    for b in range(gb):
        _set_vec(vec_ref, 0, (rw * _mod_row(gt_ref, seq0, b)) * gpost_ref[...])

        def blk(r0, b=b):
            o = o_ref[b, r0:r0 + ROW_BLK, :]
            r = lax.rsqrt(jnp.mean(o * o, axis=-1, keepdims=True) + NORM_EPS)
            o_ref[b, r0:r0 + ROW_BLK, :] = x_ref[b, r0:r0 + ROW_BLK, :] + (o * r) * vec_ref[0]
        _row_loop(rb, blk, static=True)


def _adaln_body(c_ref, w_ref, b_ref, o_ref):
    a = _silu(c_ref[...]).astype(BF16)
    o_ref[...] = _dot(a, w_ref[...].astype(BF16)) + b_ref[...]


def _adaln_mod(c_all, w_mod, b_mod):
    L, D, N = w_mod.shape
    nm = N // D
    P = c_all.shape[0]
    return pl.pallas_call(
        _adaln_body,
        grid=(L, nm),
        in_specs=[pl.BlockSpec((P, D), lambda l, m: (0, 0)),
                  pl.BlockSpec((None, D, D), lambda l, m: (l, 0, m)),
                  pl.BlockSpec((None, 1, D), lambda l, m: (l * nm + m, 0, 0))],
        out_specs=pl.BlockSpec((None, P, D), lambda l, m: (l * nm + m, 0, 0)),
        out_shape=jax.ShapeDtypeStruct((L * nm, P, D), F32),
        compiler_params=_cparams(("arbitrary", "arbitrary")),
        name="adaln_mod",
    )(c_all, w_mod, b_mod.reshape(L * nm, 1, D))


def _mod_specs(layer, sub, kinds, D):
    specs = []
    for k in kinds:
        idx = layer * 3 * N_SUB + 3 * sub + k
        specs.append(pl.BlockSpec((None, SEQ_PAD, D), lambda *_, idx=idx: (idx, 0, 0)))
    return specs


def _ffn_body(x_ref, sh_ref, sc_ref, gt_ref, gpre_ref, gpost_ref, wg_ref, wu_ref, wd_ref, *rest,
              row0, n_chunks, emit):
    if emit:
        o_ref, wgo_ref, wuo_ref, wdo_ref, h_ref, vec_ref = rest
    else:
        o_ref, h_ref, vec_ref = rest
    j = pl.program_id(2)
    gb, rb, _ = x_ref.shape
    seq0 = row0 + pl.program_id(0) * gb

    @pl.when(j == 0)
    def _():
        _prenorm_rows(x_ref, h_ref, vec_ref, gpre_ref, sh_ref, sc_ref, seq0, zero_ref=o_ref)

    if emit:
        wg, wu, wd = wg_ref[...].astype(BF16), wu_ref[...].astype(BF16), wd_ref[...].astype(BF16)
        wgo_ref[...] = wg
        wuo_ref[...] = wu
        wdo_ref[...] = wd
    else:
        wg, wu, wd = wg_ref[...], wu_ref[...], wd_ref

    h = h_ref[...]
    act = (_silu(_dot(h, wg)) * _dot(h, wu)).astype(BF16)
    dn = min(FFN_DOWN_TILE, o_ref.shape[2])
    for n0 in range(0, o_ref.shape[2], dn):
        y = _dot(act, wd[:, n0:n0 + dn])
        for b in range(gb):
            o_ref[b, :, n0:n0 + dn] += y[b * rb:(b + 1) * rb, :]

    @pl.when(j == n_chunks - 1)
    def _():
        _postnorm_rows(o_ref, x_ref, vec_ref, gpost_ref, gt_ref, seq0, FFN_RES)


def _ffn(x, mod, row0, layer, sub, g_pre, g_post, weights):
    G, R, D = x.shape
    emit = weights[0] == "f32"
    tf = FFN_FF_TILE
    gb, rb = _row_blocks(G, R, FFN_ROWS)
    grid_rows = (G // gb, R // rb)
    if emit:
        _, w_gu, w_down, ffn_idx = weights
        dff = w_down.shape[2]
        n_chunks = dff // tf
        assert grid_rows == (1, 1)
        wspecs = [pl.BlockSpec((None, None, D, tf), lambda g, t, j: (layer, ffn_idx, 0, j)),
                  pl.BlockSpec((None, None, D, tf), lambda g, t, j: (layer, ffn_idx, 0, j + n_chunks)),
                  pl.BlockSpec((None, None, tf, D), lambda g, t, j: (layer, ffn_idx, j, 0))]
        wargs = [w_gu, w_gu, w_down]
    else:
        _, wg, wu, wd = weights
        dff = wd.shape[0]
        n_chunks = dff // tf
        wargs = [wg, wu, wd]
    copy_specs = [pl.BlockSpec((D, tf), lambda g, t, j: (0, j)),
                  pl.BlockSpec((D, tf), lambda g, t, j: (0, j)),
                  pl.BlockSpec((tf, D), lambda g, t, j: (j, 0))]
    if not emit:
        wspecs = copy_specs
    gidx = layer * N_SUB + sub
    body = functools.partial(_ffn_body, row0=row0, n_chunks=n_chunks, emit=emit)
    xspec = pl.BlockSpec((gb, rb, D), lambda g, t, j: (g, t, 0))
    out_specs, out_shape = [xspec], [jax.ShapeDtypeStruct((G, R, D), F32)]
    if emit:
        out_specs += copy_specs
        out_shape += [jax.ShapeDtypeStruct((D, dff), BF16), jax.ShapeDtypeStruct((D, dff), BF16),
                      jax.ShapeDtypeStruct((dff, D), BF16)]
    outs = pl.pallas_call(
        body,
        grid=grid_rows + (n_chunks,),
        in_specs=[xspec] + _mod_specs(layer, sub, (0, 1, 2), D) + [
            pl.BlockSpec((None, 1, D), lambda g, t, j: (gidx, 0, 0)),
            pl.BlockSpec((None, 1, D), lambda g, t, j: (gidx, 0, 0))] + wspecs,
        out_specs=out_specs,
        out_shape=out_shape,
        scratch_shapes=[pltpu.VMEM((gb * rb, D), BF16), pltpu.VMEM((2, ROW_BLK, D), F32)],
        compiler_params=_cparams(("arbitrary", "arbitrary", "arbitrary"), V7X_VMEM_LIMIT_FFN_BYTES),
        name="ffn_cast" if emit else "ffn",
    )(x, mod, mod, mod, g_pre, g_post, *wargs)
    return (outs[0], tuple(outs[1:])) if emit else (outs[0], None)


def _qkv_body(x_ref, sh_ref, sc_ref, gpre_ref, w_ref, qkv_ref, k_ref, v_ref, h_ref, vec_ref, *, row0, nb, q_scale):
    n = pl.program_id(2)
    gb, rb, _ = x_ref.shape
    seq0 = row0 + pl.program_id(0) * gb

    @pl.when(n == 0)
    def _():
        _prenorm_rows(x_ref, h_ref, vec_ref, gpre_ref, sh_ref, sc_ref, seq0)

    z = _dot(h_ref[...], w_ref[...])
    zs = z * jnp.where(n < nb, q_scale, 1.0)
    for b in range(gb):
        qkv_ref[b] = zs[b * rb:(b + 1) * rb, :].astype(BF16)

    @pl.when((n >= nb) & (n < 2 * nb))
    def _():
        for b in range(gb):
            k_ref[b] = z[b * rb:(b + 1) * rb, :]

    @pl.when(n >= 2 * nb)
    def _():
        for b in range(gb):
            v_ref[b] = z[b * rb:(b + 1) * rb, :]


def _qkv_proj(x, mod, row0, layer, g_pre, w, widx, q_scale):
    G, R, D = x.shape
    tn = PRE_N_TILE
    nb = D // tn
    gb, rb = _row_blocks(G, R, QKV_ROWS)
    gidx = layer * N_SUB + 1
    body = functools.partial(_qkv_body, row0=row0, nb=nb, q_scale=q_scale)
    kspec = pl.BlockSpec((gb, rb, tn), lambda g, t, n: (g, t, jnp.clip(n - nb, 0, nb - 1)))
    vspec = pl.BlockSpec((gb, rb, tn), lambda g, t, n: (g, t, jnp.clip(n - 2 * nb, 0, nb - 1)))
    return pl.pallas_call(
        body,
        grid=(G // gb, R // rb, 3 * nb),
        in_specs=[pl.BlockSpec((gb, rb, D), lambda g, t, n: (g, t, 0))] + _mod_specs(layer, 1, (0, 1), D) + [
            pl.BlockSpec((None, 1, D), lambda g, t, n: (gidx, 0, 0)),
            pl.BlockSpec((None, D, tn), lambda g, t, n: (widx, 0, n))],
        out_specs=[pl.BlockSpec((gb, rb, tn), lambda g, t, n: (g, t, n)), kspec, vspec],
        out_shape=[jax.ShapeDtypeStruct((G, R, 3 * D), BF16), jax.ShapeDtypeStruct((G, R, D), F32),
                   jax.ShapeDtypeStruct((G, R, D), F32)],
        scratch_shapes=[pltpu.VMEM((gb * rb, D), BF16), pltpu.VMEM((2, ROW_BLK, D), F32)],
        compiler_params=_cparams(("arbitrary", "arbitrary", "arbitrary")),
        name="qkv_proj",
    )(x, mod, mod, g_pre, w)


def _mm_post_body(y_ref, w_ref, x_ref, gt_ref, gpost_ref, *rest, row0, has_bias, has_ln):
    rest = list(rest)
    b_ref = rest.pop(0) if has_bias else None
    lng_ref, lnb_ref = (rest.pop(0), rest.pop(0)) if has_ln else (None, None)
    o_ref = rest.pop(0)
    a_ref = rest.pop(0) if has_ln else None
    vec_ref = rest.pop(0)
    gb, rb, _ = x_ref.shape
    seq0 = row0 + pl.program_id(0) * gb

    if has_ln:
        _set_vec(vec_ref, 0, lng_ref[...])
        _set_vec(vec_ref, 1, lnb_ref[...])
        for b in range(gb):
            def blk(r0, b=b):
                y = y_ref[b, r0:r0 + ROW_BLK, :]
                yc = y - jnp.mean(y, axis=-1, keepdims=True)
                n = yc * lax.rsqrt(jnp.mean(yc * yc, axis=-1, keepdims=True) + LN_EPS) * vec_ref[0] + vec_ref[1]
                a_ref[b * rb + r0:b * rb + r0 + ROW_BLK, :] = _silu(n).astype(BF16)
            _row_loop(rb, blk, static=True)
        a = a_ref[...]
    else:
        a = y_ref[...].reshape(gb * rb, y_ref.shape[2])

    o = _dot(a, w_ref[...])
    if has_bias:
        o = o + b_ref[...]
    for b in range(gb):
        o_ref[b] = o[b * rb:(b + 1) * rb, :]
    _postnorm_rows(o_ref, x_ref, vec_ref, gpost_ref, gt_ref, seq0, 1.0)


def _mm_post(y, x, mod, row0, layer, g_post, w, widx, bias=None, ln=None):
    G, R, D = x.shape
    Din = y.shape[2]
    gb, rb = _row_blocks(G, R, MM_ROWS)
    gidx = layer * N_SUB + 1
    has_bias, has_ln = bias is not None, ln is not None
    assert (Din == D) if has_ln else (y.dtype == BF16)
    body = functools.partial(_mm_post_body, row0=row0, has_bias=has_bias, has_ln=has_ln)
    xspec = pl.BlockSpec((gb, rb, D), lambda g, t: (g, t, 0))
    in_specs = [pl.BlockSpec((gb, rb, Din), lambda g, t: (g, t, 0)),
                pl.BlockSpec((None, Din, D), lambda g, t: (widx, 0, 0)),
                xspec] + _mod_specs(layer, 1, (2,), D) + [
        pl.BlockSpec((None, 1, D), lambda g, t: (gidx, 0, 0))]
    args = [y, w, x, mod, g_post]
    if has_bias:
        in_specs.append(pl.BlockSpec((None, 1, D), lambda g, t: (widx, 0, 0)))
        args.append(bias.reshape(bias.shape[0], 1, D))
    if has_ln:
        for p in ln:
            in_specs.append(pl.BlockSpec((None, 1, Din), lambda g, t: (widx, 0, 0)))
            args.append(p.reshape(p.shape[0], 1, Din))
    scratch = ([pltpu.VMEM((gb * rb, Din), BF16)] if has_ln else []) + [pltpu.VMEM((2, ROW_BLK, D), F32)]
    return pl.pallas_call(
        body,
        grid=(G // gb, R // rb),
        in_specs=in_specs,
        out_specs=xspec,
        out_shape=jax.ShapeDtypeStruct((G, R, D), F32),
        scratch_shapes=scratch,
        compiler_params=_cparams(("arbitrary", "arbitrary")),
        name="mm_post",
    )(*args)


def _conv_stage(ext_ref, carry_ref, hist_ref, nh_ref, slot, b, c, t, s, n_sub, new_rows, W):
    sr = new_rows.shape[0]
    pad = ext_ref.shape[2] - sr
    lo = pad - (W - 1)
    if s == 0:
        @pl.when(t == 0)
        def _():
            ext_ref[slot, b, lo:pad, :] = hist_ref[b]

        @pl.when(t > 0)
        def _():
            ext_ref[slot, b, lo:pad, :] = carry_ref[c, b, lo:pad, :]
    else:
        ext_ref[slot, b, lo:pad, :] = ext_ref[1 - slot, b, lo + sr:pad + sr, :]
    ext_ref[slot, b, pad:pad + sr, :] = new_rows
    if s == n_sub - 1:
        tail = ext_ref[slot, b, lo + sr:pad + sr, :]
        carry_ref[c, b, lo:pad, :] = tail
        tc = ext_ref.shape[3]
        nh_ref[b, :, pl.ds(pl.multiple_of(c * tc, tc), tc)] = tail
    return lo


def _front_a_body(x_ref, sh_ref, sc_ref, gpre_ref, wb_ref, wc_ref, wx_ref, hist_ref, wconv_ref,
                  y_ref, nh_ref, h_ref, vec_ref, ext_ref, carry_ref, *, row0, n_sub):
    t, c = pl.program_id(1), pl.program_id(2)
    gb, rb, _ = x_ref.shape
    W = wconv_ref.shape[0]
    seq0 = row0 + pl.program_id(0) * gb

    @pl.when(c == 0)
    def _():
        _prenorm_rows(x_ref, h_ref, vec_ref, gpre_ref, sh_ref, sc_ref, seq0)

    sr = rb // n_sub
    rs = min(CONV_SUB_ROWS, sr)
    for b in range(gb):
        for s in range(n_sub):
            slot = s % 2
            h = h_ref[b * rb + s * sr:b * rb + (s + 1) * sr, :]
            zc, zx = _dot(h, wc_ref[...]), _dot(h, wx_ref[...])
            lo = _conv_stage(ext_ref, carry_ref, hist_ref, nh_ref, slot, b, c, t, s, n_sub, zc * zx, W)
            zb = _dot(h, wb_ref[...])
            for r0 in range(0, sr, rs):
                acc = wconv_ref[0:1, :] * ext_ref[slot, b, lo + r0:lo + r0 + rs, :]
                for j in range(1, W):
                    acc = acc + wconv_ref[j:j + 1, :] * ext_ref[slot, b, lo + j + r0:lo + j + r0 + rs, :]
                y_ref[b, s * sr + r0:s * sr + r0 + rs, :] = (zb[r0:r0 + rs] * acc).astype(y_ref.dtype)


def _front_b_body(x_ref, sh_ref, sc_ref, gpre_ref, wa_ref, wg_ref, ba_ref, bg_ref, hist_ref, wconv_ref, bdw_ref,
                  y_ref, nh_ref, h_ref, vec_ref, ext_ref, carry_ref, ph_ref, *, row0, n_sub):
    t, c = pl.program_id(1), pl.program_id(2)
    gb, rb, _ = x_ref.shape
    W = wconv_ref.shape[0]
    seq0 = row0 + pl.program_id(0) * gb

    @pl.when(c == 0)
    def _():
        _prenorm_rows(x_ref, h_ref, vec_ref, gpre_ref, sh_ref, sc_ref, seq0)

    sr = rb // n_sub
    rs = min(CONV_SUB_ROWS, sr)
    n_shift = ph_ref.shape[2]
    for b in range(gb):
        for s in range(n_sub):
            slot = s % 2
            h = h_ref[b * rb + s * sr:b * rb + (s + 1) * sr, :]
            u = (_dot(h, wa_ref[...]) + ba_ref[...]) * jax.nn.sigmoid(_dot(h, wg_ref[...]) + bg_ref[...])
            lo = _conv_stage(ext_ref, carry_ref, hist_ref, nh_ref, slot, b, c, t, s, n_sub, u, W)
            for p in range(1, SUBLANES):
                ph_ref[slot, p - 1] = ext_ref[slot, b, p:p + n_shift, :]
            for r0 in range(0, sr, rs):
                acc = None
                for j in range(W):
                    a, p = divmod(lo + j, SUBLANES)
                    start = r0 + a * SUBLANES
                    win = (ext_ref[slot, b, start:start + rs, :] if p == 0
                           else ph_ref[slot, p - 1, start:start + rs, :])
                    term = wconv_ref[j:j + 1, :] * win
                    acc = term if acc is None else acc + term
                y_ref[b, s * sr + r0:s * sr + r0 + rs, :] = acc + bdw_ref[...]


def _mixer_front(kind, x, mod, row0, layer, g_pre, w, widx, hist, wconv, bias=None, bdw=None):
    G, R, D = x.shape
    nsplit = 3 if kind == "a" else 2
    W = wconv.shape[1]
    gb, rb = _row_blocks(G, R, FRONT_ROWS)
    n_sub = FRONT_SUB_TILES if rb % (FRONT_SUB_TILES * CONV_SUB_ROWS) == 0 else 1
    sr = rb // n_sub
    assert sr >= W - 1 and sr % min(CONV_SUB_ROWS, sr) == 0
    tc = FRONT_COLS_A if kind == "a" else FRONT_COLS_B
    nc = D // tc
    pad = -(-(W - 1) // SUBLANES) * SUBLANES
    gidx = layer * N_SUB + 1
    wspecs = [pl.BlockSpec((None, D, tc), lambda g, t, c, s=s: (widx, 0, s * nc + c)) for s in range(nsplit)]
    hist_spec = pl.BlockSpec((gb, W - 1, tc), lambda g, t, c: (g, 0, c))
    in_specs = [pl.BlockSpec((gb, rb, D), lambda g, t, c: (g, t, 0))] + _mod_specs(layer, 1, (0, 1), D) + [
        pl.BlockSpec((None, 1, D), lambda g, t, c: (gidx, 0, 0))] + wspecs
    args = [x, mod, mod, g_pre] + [w] * nsplit
    scratch = [pltpu.VMEM((gb * rb, D), BF16), pltpu.VMEM((2, ROW_BLK, D), F32),
               pltpu.VMEM((2, gb, pad + sr, tc), F32), pltpu.VMEM((nc, gb, pad, tc), F32)]
    if kind == "a":
        body, out_dtype = _front_a_body, BF16
    else:
        body, out_dtype = _front_b_body, F32
        in_specs += [pl.BlockSpec((None, 1, tc), lambda g, t, c, s=s: (widx, 0, s * nc + c)) for s in range(nsplit)]
        args += [bias.reshape(bias.shape[0], 1, nsplit * D)] * nsplit
        scratch.append(pltpu.VMEM((2, SUBLANES - 1, pad + sr - SUBLANES, tc), F32))
    in_specs += [hist_spec, pl.BlockSpec((None, W, tc), lambda g, t, c: (widx, 0, c))]
    args += [hist, wconv]
    if kind == "b":
        in_specs.append(pl.BlockSpec((None, 1, tc), lambda g, t, c: (widx, 0, c)))
        args.append(bdw.reshape(bdw.shape[0], 1, D))
    return pl.pallas_call(
        functools.partial(body, row0=row0, n_sub=n_sub),
        grid=(G // gb, R // rb, nc),
        in_specs=in_specs,
        out_specs=[pl.BlockSpec((gb, rb, tc), lambda g, t, c: (g, t, c)),
                   pl.BlockSpec((gb, W - 1, D), lambda g, t, c: (g, 0, 0))],
        out_shape=[jax.ShapeDtypeStruct((G, R, D), out_dtype),
                   jax.ShapeDtypeStruct((G, W - 1, D), F32)],
        scratch_shapes=scratch,
        compiler_params=_cparams(("arbitrary", "arbitrary", "arbitrary")),
        name="front_" + kind,
    )(*args)


def _last_visible_tile(q_pos, tk):
    return ((q_pos // CHUNK) * CHUNK + CHUNK - 1) // tk


def _lambda(lam_ref, lam_init):
    lp = lam_ref[...]
    return (jnp.exp(jnp.sum(lp[0:1] * lp[1:2], axis=-1, keepdims=True))
            - jnp.exp(jnp.sum(lp[2:3] * lp[3:4], axis=-1, keepdims=True)) + lam_init)


def _diff_out(o1, o2, lam, subg, lam_init):
    o = o1 - lam * o2
    r = o * lax.rsqrt(jnp.mean(o * o, axis=0, keepdims=True) + SUBLN_EPS) * subg
    return (r * (1.0 - lam_init)).T


def _attn_body(qi_ref, ki_ref, last_ref, slope_ref, lam_ref, subg_ref, q_ref, k_ref, v_ref, o_ref,
               m_ref, l_ref, acc_ref, ramp_ref, *, pos0, n_keys, lam_init):
    hp, s = pl.program_id(1), pl.program_id(2)
    qi, ki = qi_ref[s], ki_ref[s]
    tq = q_ref.shape[0]
    tk = k_ref.shape[0]
    dh2 = subg_ref.shape[0]
    dh = dh2 // 2
    n_heads = q_ref.shape[1] // dh2
    shift = int(math.log2(CHUNK))
    slopes = [slope_ref[hp * n_heads + hh] for hh in range(n_heads)]

    @pl.when(s == 0)
    def _():
        d = (lax.broadcasted_iota(jnp.int32, (tk, tq), 1) - lax.broadcasted_iota(jnp.int32, (tk, tq), 0)).astype(F32)
        for hh in range(n_heads):
            ramp_ref[hh] = -slopes[hh] * d

    @pl.when(ki == 0)
    def _():
        m_ref[...] = jnp.full(m_ref.shape, NEG_INF, F32)
        l_ref[...] = jnp.zeros(l_ref.shape, F32)
        acc_ref[...] = jnp.zeros(acc_ref.shape, F32)

    q_lo = pos0 + qi * tq
    k_lo = ki * tk

    def raw_scores(hh, c, l0=0):
        col = hh * dh2 + c * dh
        return lax.dot_general(k_ref[:, col:col + dh], q_ref[l0:, col:col + dh],
                               (((1,), (1,)), ((), ())), preferred_element_type=F32)

    def online_update(hh, c, t, cst, l0=0):
        i = 2 * hh + c
        m_old = m_ref[i, :, l0:]
        m_new = jnp.maximum(m_old, jnp.max(t, axis=0, keepdims=True) + cst)
        alpha = jnp.exp2(m_old - m_new)
        p = jnp.exp2(t + (cst - m_new))
        l_ref[i, :, l0:] = alpha * l_ref[i, :, l0:] + jnp.sum(p, axis=0, keepdims=True)
        pv = lax.dot_general(v_ref[:, hh * dh2:(hh + 1) * dh2], p.astype(BF16), (((0,), (0,)), ((), ())),
                             preferred_element_type=F32)
        acc_ref[i, :, l0:] = alpha * acc_ref[i, :, l0:] + pv
        m_ref[i, :, l0:] = m_new

    def masked_update(l0):
        qpos = q_lo + l0 + lax.broadcasted_iota(jnp.int32, (1, tq - l0), 1)
        kpos = k_lo + lax.broadcasted_iota(jnp.int32, (tk, 1), 0)
        visible = (jnp.right_shift(kpos, shift) <= jnp.right_shift(qpos, shift)) & (kpos < n_keys)
        dist = jnp.abs(qpos - kpos).astype(F32)
        for hh in range(n_heads):
            bias = -slopes[hh] * dist
            for c in range(2):
                online_update(hh, c, jnp.where(visible, raw_scores(hh, c, l0) + bias, NEG_INF), 0.0, l0)

    all_past = k_lo + tk - 1 < q_lo
    half = tq // 2 if (tq // 2) % LANES == 0 and tk % CHUNK == 0 else None
    late = (k_lo >= q_lo + half) if half is not None else False

    @pl.when(all_past)
    def _():
        for hh in range(n_heads):
            cst = -slopes[hh] * (q_lo - k_lo).astype(F32)
            for c in range(2):
                online_update(hh, c, raw_scores(hh, c) + ramp_ref[hh], cst)

    if half is None:
        @pl.when(jnp.logical_not(all_past))
        def _():
            masked_update(0)
    else:
        @pl.when(jnp.logical_not(all_past) & jnp.logical_not(late))
        def _():
            masked_update(0)

        @pl.when(jnp.logical_not(all_past) & late)
        def _():
            masked_update(half)

    @pl.when(last_ref[s] == 1)
    def _():
        lam = _lambda(lam_ref, lam_init)
        for hh in range(n_heads):
            o_ref[:, hh * dh2:(hh + 1) * dh2] = _diff_out(
                acc_ref[2 * hh] / l_ref[2 * hh], acc_ref[2 * hh + 1] / l_ref[2 * hh + 1], lam,
                subg_ref[...], lam_init).astype(o_ref.dtype)


def _diff_attn(q_arr, q_off, k_arr, k_off, v_arr, v_off, n_keys, pos0, lam_params, subln_g, idx, lam_init,
               slopes):
    B, Tq, _ = q_arr.shape
    Tk = k_arr.shape[1]
    dh2 = subln_g.shape[1]
    H = N_HEADS
    tq = min(ATTN_Q_TILE, Tq)
    tk = Tk if Tk <= 2 * ATTN_K_TILE + LANES else ATTN_K_TILE
    assert Tq % tq == 0 and Tk % tk == 0 and pos0 + Tq <= n_keys <= Tk
    assert 1 << int(math.log2(CHUNK)) == CHUNK
    hps = ATTN_HEADS_PER_STEP if H % ATTN_HEADS_PER_STEP == 0 else 1
    assert q_off % hps == 0 and k_off % hps == 0 and v_off % hps == 0
    nk = Tk // tk
    qis, kis, lasts = [], [], []
    for qi in range(Tq // tq):
        last = min(_last_visible_tile(pos0 + (qi + 1) * tq - 1, tk), nk - 1)
        for ki in range(last + 1):
            qis.append(qi)
            kis.append(ki)
            lasts.append(int(ki == last))
    sched = [jnp.asarray(np.asarray(a, np.int32)) for a in (qis, kis, lasts)]

    body = functools.partial(_attn_body, pos0=pos0, n_keys=n_keys, lam_init=lam_init)
    grid_spec = pltpu.PrefetchScalarGridSpec(
        num_scalar_prefetch=3,
        grid=(B, H // hps, len(qis)),
        in_specs=[pl.BlockSpec(memory_space=pltpu.SMEM),
                  pl.BlockSpec((None, 4, dh2 // 2), lambda b, h, s, qi, ki, la: (idx, 0, 0)),
                  pl.BlockSpec((None, dh2, 1), lambda b, h, s, qi, ki, la: (idx, 0, 0)),
                  pl.BlockSpec((None, tq, hps * dh2), lambda b, h, s, qi, ki, la: (b, qi[s], q_off // hps + h)),
                  pl.BlockSpec((None, tk, hps * dh2), lambda b, h, s, qi, ki, la: (b, ki[s], k_off // hps + h)),
                  pl.BlockSpec((None, tk, hps * dh2), lambda b, h, s, qi, ki, la: (b, ki[s], v_off // hps + h))],
        out_specs=pl.BlockSpec((None, tq, hps * dh2), lambda b, h, s, qi, ki, la: (b, qi[s], h)),
        scratch_shapes=[pltpu.VMEM((2 * hps, 1, tq), F32), pltpu.VMEM((2 * hps, 1, tq), F32),
                        pltpu.VMEM((2 * hps, dh2, tq), F32), pltpu.VMEM((hps, tk, tq), F32)])
    return pl.pallas_call(
        body,
        grid_spec=grid_spec,
        out_shape=jax.ShapeDtypeStruct((B, Tq, H * dh2), BF16),
        compiler_params=_cparams(("arbitrary", "arbitrary", "arbitrary")),
        name="diff_attn",
    )(*sched, slopes * LOG2E, lam_params, subln_g.reshape(subln_g.shape[0], dh2, 1), q_arr, k_arr, v_arr)


def _cached_attn_body(slope_ref, lam_ref, subg_ref, qkv_ref, ck1_ref, ck2_ref, cv1_ref, cv2_ref, o_ref,
                      *, past, lam_init):
    R = qkv_ref.shape[0]
    dh = ck1_ref.shape[1]
    dh2 = 2 * dh
    H = ck1_ref.shape[0] // past
    D = H * dh2
    shift = int(math.log2(CHUNK))
    trans_b = (((1,), (1,)), ((), ()))

    d_cache = (past + lax.broadcasted_iota(jnp.int32, (R, past), 0)
               - lax.broadcasted_iota(jnp.int32, (R, past), 1)).astype(F32)
    qpos = past + lax.broadcasted_iota(jnp.int32, (R, 1), 0)
    kpos = past + lax.broadcasted_iota(jnp.int32, (1, R), 1)
    vis_new = jnp.right_shift(kpos, shift) <= jnp.right_shift(qpos, shift)
    d_new = jnp.abs(qpos - kpos).astype(F32)
    lam = _lambda(lam_ref, lam_init)

    for h in range(H):
        slope = slope_ref[h]
        head_rows = pl.ds(h, past, stride=H)
        kc = (ck1_ref[head_rows, :].astype(BF16), ck2_ref[head_rows, :].astype(BF16))
        vc = jnp.concatenate([cv1_ref[head_rows, :].astype(BF16), cv2_ref[head_rows, :].astype(BF16)], axis=-1)
        q = qkv_ref[:, h * dh2:(h + 1) * dh2]
        kn = qkv_ref[:, D + h * dh2:D + (h + 1) * dh2]
        vn = qkv_ref[:, 2 * D + h * dh2:2 * D + (h + 1) * dh2]
        outs = []
        for c in range(2):
            qc = q[:, c * dh:(c + 1) * dh]
            t1 = lax.dot_general(qc, kc[c], trans_b, preferred_element_type=F32) - slope * d_cache
            t2 = lax.dot_general(qc, kn[:, c * dh:(c + 1) * dh], trans_b, preferred_element_type=F32) - slope * d_new
            t2 = jnp.where(vis_new, t2, NEG_INF)
            m = jnp.maximum(jnp.max(t1, axis=-1, keepdims=True), jnp.max(t2, axis=-1, keepdims=True))
            p1 = jnp.exp2(t1 - m)
            p2 = jnp.exp2(t2 - m)
            l = jnp.sum(p1, axis=-1, keepdims=True) + jnp.sum(p2, axis=-1, keepdims=True)
            outs.append((_dot(p1.astype(BF16), vc) + _dot(p2.astype(BF16), vn)) / l)
        o = outs[0] - lam * outs[1]
        r = o * lax.rsqrt(jnp.mean(o * o, axis=-1, keepdims=True) + SUBLN_EPS) * subg_ref[...]
        o_ref[:, h * dh2:(h + 1) * dh2] = (r * (1.0 - lam_init)).astype(o_ref.dtype)


def _cached_attn(qkv, cache_k, cache_v, idx, lam_params, subln_g, lam_init, slopes):
    B, R, _ = qkv.shape
    n_c, _, past, H, dh2 = cache_k.shape
    D = H * dh2
    body = functools.partial(_cached_attn_body, past=past, lam_init=lam_init)
    cspecs = [pl.BlockSpec((None, None, past * H, dh2 // 2), lambda b, half=half: (idx, b, 0, half))
              for half in (0, 1)]
    return pl.pallas_call(
        body,
        grid=(B,),
        in_specs=[pl.BlockSpec(memory_space=pltpu.SMEM),
                  pl.BlockSpec((None, 4, dh2 // 2), lambda b: (idx, 0, 0)),
                  pl.BlockSpec((None, 1, dh2), lambda b: (idx, 0, 0)),
                  pl.BlockSpec((None, R, 3 * D), lambda b: (b, 0, 0)),
                  ] + cspecs + cspecs,
        out_specs=pl.BlockSpec((None, R, D), lambda b: (b, 0, 0)),
        out_shape=jax.ShapeDtypeStruct((B, R, D), BF16),
        compiler_params=_cparams(("arbitrary",)),
        name="cached_attn",
    )(slopes * LOG2E, lam_params, subln_g.reshape(subln_g.shape[0], 1, dh2), qkv,
      *([cache_k.reshape(n_c, B, past * H, dh2)] * 2 + [cache_v.reshape(n_c, B, past * H, dh2)] * 2))


def _trunk(x, row0, mod, hist_a, hist_b, hist_k, hist_v, pos0, p, ffn_weights):
    G, R, D = x.shape
    depth = p["depth"]
    slopes = jnp.asarray([2.0 ** (-8.0 * (h + 1) / N_HEADS) for h in range(N_HEADS)], dtype=F32)
    new_a, new_b, new_k, new_v, copies = [], [], [], [], {}
    for i in range(depth):
        x, copies[i, 0] = _ffn(x, mod, row0, i, 0, p["g_pre"], p["g_post"], ffn_weights[i, 0])
        kind, idx = i % N_MIXERS, i // N_MIXERS
        if kind == 0:
            y, st = _mixer_front("a", x, mod, row0, i, p["g_pre"], p["sc_w_in"], idx, hist_a[idx], p["sc_w_conv"])
            new_a.append(st)
            x = _mm_post(y, x, mod, row0, i, p["g_post"], p["sc_w_out"], idx)
        elif kind == 1:
            y, st = _mixer_front("b", x, mod, row0, i, p["g_pre"], p["cm_w_pw1"], idx, hist_b[idx], p["cm_w_dw"],
                                 bias=p["cm_b_pw1"], bdw=p["cm_b_dw"])
            new_b.append(st)
            x = _mm_post(y, x, mod, row0, i, p["g_post"], p["cm_w_pw2"], idx, bias=p["cm_b_pw2"],
                         ln=(p["cm_ln_g"], p["cm_ln_b"]))
        else:
            lam_init = 0.8 - 0.6 * math.exp(-0.3 * i)
            dh2 = p["attn_subln_g"].shape[1]
            qkv, k_rows, v_rows = _qkv_proj(x, mod, row0, i, p["g_pre"], p["attn_w_qkv"], idx,
                                            (dh2 // 2) ** -0.5 * LOG2E)
            new_k.append(k_rows)
            new_v.append(v_rows)
            nblk = D // dh2
            if hist_k is None:
                o = _diff_attn(qkv, 0, qkv, nblk, qkv, 2 * nblk, R, pos0, p["attn_lambda"], p["attn_subln_g"],
                               idx, lam_init, slopes)
            else:
                assert pos0 == hist_k.shape[2]
                o = _cached_attn(qkv, hist_k, hist_v, idx, p["attn_lambda"], p["attn_subln_g"], lam_init, slopes)
            x = _mm_post(o, x, mod, row0, i, p["g_post"], p["attn_w_o"], idx)
        x, copies[i, 1] = _ffn(x, mod, row0, i, 2, p["g_pre"], p["g_post"], ffn_weights[i, 1])
    return (x, jnp.stack(new_a), jnp.stack(new_b), jnp.stack(new_k), jnp.stack(new_v)), copies


def kernel(x_prompt, x_sample, state_conv_a, state_conv_b, cache_k, cache_v, c_prompt, c_sample, w_mod, b_mod, g_pre, g_post, ffn_w_gu, ffn_w_down, sc_w_in, sc_w_conv, sc_w_out, cm_w_pw1, cm_b_pw1, cm_w_dw, cm_b_dw, cm_ln_g, cm_ln_b, cm_w_pw2, cm_b_pw2, attn_w_qkv, attn_lambda, attn_subln_g, attn_w_o):
    Bp, T, D = x_prompt.shape
    Bs = x_sample.shape[0]
    depth = g_pre.shape[0]
    n_a, n_b = state_conv_a.shape[0], state_conv_b.shape[0]
    assert Bp + Bs <= SEQ_PAD
    H, dh2 = N_HEADS, attn_subln_g.shape[1]

    c_all = jnp.concatenate([c_prompt, c_sample, jnp.zeros((SEQ_PAD - Bp - Bs, D), F32)], axis=0)
    mod = _adaln_mod(c_all, w_mod, b_mod)

    p = dict(
        depth=depth,
        g_pre=g_pre.reshape(depth * N_SUB, 1, D), g_post=g_post.reshape(depth * N_SUB, 1, D),
        sc_w_in=sc_w_in.astype(BF16), sc_w_conv=sc_w_conv, sc_w_out=sc_w_out.astype(BF16),
        cm_w_pw1=cm_w_pw1.astype(BF16), cm_b_pw1=cm_b_pw1, cm_w_dw=cm_w_dw, cm_b_dw=cm_b_dw,
        cm_ln_g=cm_ln_g, cm_ln_b=cm_ln_b, cm_w_pw2=cm_w_pw2.astype(BF16), cm_b_pw2=cm_b_pw2,
        attn_w_qkv=attn_w_qkv.astype(BF16), attn_lambda=attn_lambda, attn_subln_g=attn_subln_g,
        attn_w_o=attn_w_o.astype(BF16))

    past = cache_k.shape[2]
    f32_weights = {(i, f): ("f32", ffn_w_gu, ffn_w_down, f) for i in range(depth) for f in range(2)}
    (y_s, a_s, b_s, k_s, v_s), copies = _trunk(x_sample, Bp, mod, state_conv_a, state_conv_b, cache_k, cache_v,
                                               past, p, f32_weights)
    bf16_weights = {key: ("bf16",) + copies[key] for key in copies}
    zeros_a = jnp.zeros((n_a, Bp) + state_conv_a.shape[2:], F32)
    zeros_b = jnp.zeros((n_b, Bp) + state_conv_b.shape[2:], F32)
    (y_p, a_p, b_p, k_p, v_p), _ = _trunk(x_prompt, 0, mod, zeros_a, zeros_b, None, None, 0, p, bf16_weights)

    def heads(a):
        return a.reshape(a.shape[:3] + (H, dh2))
    return (y_p, y_s, a_p, a_s, b_p, b_s, heads(k_p), heads(v_p), heads(k_s), heads(v_s))
```

```python
import functools
import math

import numpy as np
import jax
import jax.numpy as jnp
from jax import lax
from jax.experimental import pallas as pl
from jax.experimental.pallas import tpu as pltpu

F32 = jnp.float32
BF16 = jnp.bfloat16

N_HEADS = 8
CHUNK = 64
FFN_RES = 0.5
NORM_EPS = 1e-6
SUBLN_EPS = 1e-5
LN_EPS = 1e-5
NEG_INF = -1e30
LOG2E = math.log2(math.e)
N_SUB = 3
N_MIXERS = 3

SEQ_PAD = 16
V7X_VMEM_LIMIT_BYTES = 56 * 1024 * 1024
V7X_VMEM_LIMIT_FFN_BYTES = 60 * 1024 * 1024
SUBLANES = 8
LANES = 128
ROW_BLK = 16

FFN_ROWS = 1024
FFN_FF_TILE = 512
FFN_DOWN_TILE = 512
MM_ROWS = 512
QKV_ROWS = 1024
PRE_N_TILE = 1024
FRONT_ROWS = 1024
FRONT_COLS_A = 512
FRONT_COLS_B = 256
FRONT_SUB_TILES = 1
CONV_SUB_ROWS = 64
ATTN_Q_TILE = 1024
ATTN_K_TILE = 512
ATTN_HEADS_PER_STEP = 4


def _cparams(sem, vmem_limit_bytes=V7X_VMEM_LIMIT_BYTES):
    return pltpu.CompilerParams(dimension_semantics=sem, vmem_limit_bytes=vmem_limit_bytes)


def _row_blocks(G, R, rows):
    if R >= rows:
        assert R % rows == 0
        return 1, rows
    gb = min(G, rows // R)
    assert G % gb == 0 and R % ROW_BLK == 0
    return gb, R


def _silu(x):
    return x * jax.nn.sigmoid(x)


def _dot(a, b):
    return jnp.dot(a, b, preferred_element_type=F32)


def _mod_row(ref, seq0, b):
    return ref[pl.ds(seq0 + b, 1), :]


def _row_loop(n_rows, fn):
    for i in range(n_rows // ROW_BLK):
        fn(i * ROW_BLK)


def _set_vec(vec_ref, k, row):
    vec_ref[k] = jnp.broadcast_to(row, vec_ref.shape[1:])


def _prenorm_rows(x_ref, h_ref, vec_ref, gpre_ref, sh_ref, sc_ref, seq0, zero_ref=None):
    gb, rb, _ = x_ref.shape
    for b in range(gb):
        _set_vec(vec_ref, 0, gpre_ref[...] * (1.0 + _mod_row(sc_ref, seq0, b)))
        _set_vec(vec_ref, 1, _mod_row(sh_ref, seq0, b))

        def blk(r0, b=b):
            x = x_ref[b, r0:r0 + ROW_BLK, :]
            r = lax.rsqrt(jnp.mean(x * x, axis=-1, keepdims=True) + NORM_EPS)
            h_ref[b * rb + r0:b * rb + r0 + ROW_BLK, :] = ((x * r) * vec_ref[0] + vec_ref[1]).astype(BF16)
            if zero_ref is not None:
                zero_ref[b, r0:r0 + ROW_BLK, :] = jnp.zeros((ROW_BLK, zero_ref.shape[2]), F32)
        _row_loop(rb, blk)


def _postnorm_rows(o_ref, x_ref, vec_ref, gpost_ref, gt_ref, seq0, rw):
    gb, rb, _ = x_ref.shape
    for b in range(gb):
        _set_vec(vec_ref, 0, (rw * _mod_row(gt_ref, seq0, b)) * gpost_ref[...])

        def blk(r0, b=b):
            o = o_ref[b, r0:r0 + ROW_BLK, :]
            r = lax.rsqrt(jnp.mean(o * o, axis=-1, keepdims=True) + NORM_EPS)
            o_ref[b, r0:r0 + ROW_BLK, :] = x_ref[b, r0:r0 + ROW_BLK, :] + (o * r) * vec_ref[0]
        _row_loop(rb, blk)


def _adaln_body(c_ref, w_ref, b_ref, o_ref):
    a = _silu(c_ref[...]).astype(BF16)
    o_ref[...] = _dot(a, w_ref[...].astype(BF16)) + b_ref[...]


def _adaln_mod(c_all, w_mod, b_mod):
    L, D, N = w_mod.shape
    nm = N // D
    P = c_all.shape[0]
    return pl.pallas_call(
        _adaln_body,
        grid=(L, nm),
        in_specs=[pl.BlockSpec((P, D), lambda l, m: (0, 0)),
                  pl.BlockSpec((None, D, D), lambda l, m: (l, 0, m)),
                  pl.BlockSpec((None, 1, D), lambda l, m: (l * nm + m, 0, 0))],
        out_specs=pl.BlockSpec((None, P, D), lambda l, m: (l * nm + m, 0, 0)),
        out_shape=jax.ShapeDtypeStruct((L * nm, P, D), F32),
        compiler_params=_cparams(("arbitrary", "arbitrary")),
        name="adaln_mod",
    )(c_all, w_mod, b_mod.reshape(L * nm, 1, D))


def _mod_specs(layer, sub, kinds, D):
    specs = []
    for k in kinds:
        idx = layer * 3 * N_SUB + 3 * sub + k
        specs.append(pl.BlockSpec((None, SEQ_PAD, D), lambda *_, idx=idx: (idx, 0, 0)))
    return specs


def _ffn_body(x_ref, sh_ref, sc_ref, gt_ref, gpre_ref, gpost_ref, wg_ref, wu_ref, wd_ref, *rest,
              row0, n_chunks, emit):
    if emit:
        o_ref, wgo_ref, wuo_ref, wdo_ref, h_ref, vec_ref = rest
    else:
        o_ref, h_ref, vec_ref = rest
    j = pl.program_id(2)
    gb, rb, _ = x_ref.shape
    seq0 = row0 + pl.program_id(0) * gb

    @pl.when(j == 0)
    def _():
        _prenorm_rows(x_ref, h_ref, vec_ref, gpre_ref, sh_ref, sc_ref, seq0, zero_ref=o_ref)

    if emit:
        wg, wu, wd = wg_ref[...].astype(BF16), wu_ref[...].astype(BF16), wd_ref[...].astype(BF16)
        wgo_ref[...] = wg
        wuo_ref[...] = wu
        wdo_ref[...] = wd
    else:
        wg, wu, wd = wg_ref[...], wu_ref[...], wd_ref

    h = h_ref[...]
    act = (_silu(_dot(h, wg)) * _dot(h, wu)).astype(BF16)
    dn = min(FFN_DOWN_TILE, o_ref.shape[2])
    for n0 in range(0, o_ref.shape[2], dn):
        y = _dot(act, wd[:, n0:n0 + dn])
        for b in range(gb):
            o_ref[b, :, n0:n0 + dn] += y[b * rb:(b + 1) * rb, :]

    @pl.when(j == n_chunks - 1)
    def _():
        _postnorm_rows(o_ref, x_ref, vec_ref, gpost_ref, gt_ref, seq0, FFN_RES)


def _ffn(x, mod, row0, layer, sub, g_pre, g_post, weights):
    G, R, D = x.shape
    emit = weights[0] == "f32"
    tf = FFN_FF_TILE
    gb, rb = _row_blocks(G, R, FFN_ROWS)
    grid_rows = (G // gb, R // rb)
    if emit:
        _, w_gu, w_down, ffn_idx = weights
        dff = w_down.shape[2]
        n_chunks = dff // tf
        assert grid_rows == (1, 1)
        wspecs = [pl.BlockSpec((None, None, D, tf), lambda g, t, j: (layer, ffn_idx, 0, j)),
                  pl.BlockSpec((None, None, D, tf), lambda g, t, j: (layer, ffn_idx, 0, j + n_chunks)),
                  pl.BlockSpec((None, None, tf, D), lambda g, t, j: (layer, ffn_idx, j, 0))]
        wargs = [w_gu, w_gu, w_down]
    else:
        _, wg, wu, wd = weights
        dff = wd.shape[0]
        n_chunks = dff // tf
        wargs = [wg, wu, wd]
    copy_specs = [pl.BlockSpec((D, tf), lambda g, t, j: (0, j)),
                  pl.BlockSpec((D, tf), lambda g, t, j: (0, j)),
                  pl.BlockSpec((tf, D), lambda g, t, j: (j, 0))]
    if not emit:
        wspecs = copy_specs
    gidx = layer * N_SUB + sub
    body = functools.partial(_ffn_body, row0=row0, n_chunks=n_chunks, emit=emit)
    xspec = pl.BlockSpec((gb, rb, D), lambda g, t, j: (g, t, 0))
    out_specs, out_shape = [xspec], [jax.ShapeDtypeStruct((G, R, D), F32)]
    if emit:
        out_specs += copy_specs
        out_shape += [jax.ShapeDtypeStruct((D, dff), BF16), jax.ShapeDtypeStruct((D, dff), BF16),
                      jax.ShapeDtypeStruct((dff, D), BF16)]
    outs = pl.pallas_call(
        body,
        grid=grid_rows + (n_chunks,),
        in_specs=[xspec] + _mod_specs(layer, sub, (0, 1, 2), D) + [
            pl.BlockSpec((None, 1, D), lambda g, t, j: (gidx, 0, 0)),
            pl.BlockSpec((None, 1, D), lambda g, t, j: (gidx, 0, 0))] + wspecs,
        out_specs=out_specs,
        out_shape=out_shape,
        scratch_shapes=[pltpu.VMEM((gb * rb, D), BF16), pltpu.VMEM((2, ROW_BLK, D), F32)],
        compiler_params=_cparams(("arbitrary", "arbitrary", "arbitrary"), V7X_VMEM_LIMIT_FFN_BYTES),
        name="ffn_cast" if emit else "ffn",
    )(x, mod, mod, mod, g_pre, g_post, *wargs)
    return (outs[0], tuple(outs[1:])) if emit else (outs[0], None)


def _qkv_body(x_ref, sh_ref, sc_ref, gpre_ref, w_ref, qkv_ref, k_ref, v_ref, h_ref, vec_ref, *, row0, nb, q_scale):
    n = pl.program_id(2)
    gb, rb, _ = x_ref.shape
    seq0 = row0 + pl.program_id(0) * gb

    @pl.when(n == 0)
    def _():
        _prenorm_rows(x_ref, h_ref, vec_ref, gpre_ref, sh_ref, sc_ref, seq0)

    z = _dot(h_ref[...], w_ref[...])
    zs = z * jnp.where(n < nb, q_scale, 1.0)
    for b in range(gb):
        qkv_ref[b] = zs[b * rb:(b + 1) * rb, :].astype(BF16)

    @pl.when((n >= nb) & (n < 2 * nb))
    def _():
        for b in range(gb):
            k_ref[b] = z[b * rb:(b + 1) * rb, :]

    @pl.when(n >= 2 * nb)
    def _():
        for b in range(gb):
            v_ref[b] = z[b * rb:(b + 1) * rb, :]


def _qkv_proj(x, mod, row0, layer, g_pre, w, widx, q_scale):
    G, R, D = x.shape
    tn = PRE_N_TILE
    nb = D // tn
    gb, rb = _row_blocks(G, R, QKV_ROWS)
    gidx = layer * N_SUB + 1
    body = functools.partial(_qkv_body, row0=row0, nb=nb, q_scale=q_scale)
    kspec = pl.BlockSpec((gb, rb, tn), lambda g, t, n: (g, t, jnp.clip(n - nb, 0, nb - 1)))
    vspec = pl.BlockSpec((gb, rb, tn), lambda g, t, n: (g, t, jnp.clip(n - 2 * nb, 0, nb - 1)))
    return pl.pallas_call(
        body,
        grid=(G // gb, R // rb, 3 * nb),
        in_specs=[pl.BlockSpec((gb, rb, D), lambda g, t, n: (g, t, 0))] + _mod_specs(layer, 1, (0, 1), D) + [
            pl.BlockSpec((None, 1, D), lambda g, t, n: (gidx, 0, 0)),
            pl.BlockSpec((None, D, tn), lambda g, t, n: (widx, 0, n))],
        out_specs=[pl.BlockSpec((gb, rb, tn), lambda g, t, n: (g, t, n)), kspec, vspec],
        out_shape=[jax.ShapeDtypeStruct((G, R, 3 * D), BF16), jax.ShapeDtypeStruct((G, R, D), F32),
                   jax.ShapeDtypeStruct((G, R, D), F32)],
        scratch_shapes=[pltpu.VMEM((gb * rb, D), BF16), pltpu.VMEM((2, ROW_BLK, D), F32)],
        compiler_params=_cparams(("arbitrary", "arbitrary", "arbitrary")),
        name="qkv_proj",
    )(x, mod, mod, g_pre, w)


def _mm_post_body(y_ref, w_ref, x_ref, gt_ref, gpost_ref, *rest, row0, has_bias, has_ln):
    rest = list(rest)
    b_ref = rest.pop(0) if has_bias else None
    lng_ref, lnb_ref = (rest.pop(0), rest.pop(0)) if has_ln else (None, None)
    o_ref = rest.pop(0)
    a_ref = rest.pop(0) if has_ln else None
    vec_ref = rest.pop(0)
    gb, rb, _ = x_ref.shape
    seq0 = row0 + pl.program_id(0) * gb

    if has_ln:
        _set_vec(vec_ref, 0, lng_ref[...])
        _set_vec(vec_ref, 1, lnb_ref[...])
        for b in range(gb):
            def blk(r0, b=b):
                y = y_ref[b, r0:r0 + ROW_BLK, :]
                yc = y - jnp.mean(y, axis=-1, keepdims=True)
                n = yc * lax.rsqrt(jnp.mean(yc * yc, axis=-1, keepdims=True) + LN_EPS) * vec_ref[0] + vec_ref[1]
                a_ref[b * rb + r0:b * rb + r0 + ROW_BLK, :] = _silu(n).astype(BF16)
            _row_loop(rb, blk)
        a = a_ref[...]
    else:
        a = y_ref[...].reshape(gb * rb, y_ref.shape[2])

    o = _dot(a, w_ref[...])
    if has_bias:
        o = o + b_ref[...]
    for b in range(gb):
        o_ref[b] = o[b * rb:(b + 1) * rb, :]
    _postnorm_rows(o_ref, x_ref, vec_ref, gpost_ref, gt_ref, seq0, 1.0)


def _mm_post(y, x, mod, row0, layer, g_post, w, widx, bias=None, ln=None):
    G, R, D = x.shape
    Din = y.shape[2]
    gb, rb = _row_blocks(G, R, MM_ROWS)
    gidx = layer * N_SUB + 1
    has_bias, has_ln = bias is not None, ln is not None
    assert (Din == D) if has_ln else (y.dtype == BF16)
    body = functools.partial(_mm_post_body, row0=row0, has_bias=has_bias, has_ln=has_ln)
    xspec = pl.BlockSpec((gb, rb, D), lambda g, t: (g, t, 0))
    in_specs = [pl.BlockSpec((gb, rb, Din), lambda g, t: (g, t, 0)),
                pl.BlockSpec((None, Din, D), lambda g, t: (widx, 0, 0)),
                xspec] + _mod_specs(layer, 1, (2,), D) + [
        pl.BlockSpec((None, 1, D), lambda g, t: (gidx, 0, 0))]
    args = [y, w, x, mod, g_post]
    if has_bias:
        in_specs.append(pl.BlockSpec((None, 1, D), lambda g, t: (widx, 0, 0)))
        args.append(bias.reshape(bias.shape[0], 1, D))
    if has_ln:
        for p in ln:
            in_specs.append(pl.BlockSpec((None, 1, Din), lambda g, t: (widx, 0, 0)))
            args.append(p.reshape(p.shape[0], 1, Din))
    scratch = ([pltpu.VMEM((gb * rb, Din), BF16)] if has_ln else []) + [pltpu.VMEM((2, ROW_BLK, D), F32)]
    return pl.pallas_call(
        body,
        grid=(G // gb, R // rb),
        in_specs=in_specs,
        out_specs=xspec,
        out_shape=jax.ShapeDtypeStruct((G, R, D), F32),
        scratch_shapes=scratch,
        compiler_params=_cparams(("arbitrary", "arbitrary")),
        name="mm_post",
    )(*args)


def _conv_stage(ext_ref, carry_ref, hist_ref, nh_ref, slot, b, c, t, s, n_sub, new_rows, W):
    sr = new_rows.shape[0]
    pad = ext_ref.shape[2] - sr
    lo = pad - (W - 1)
    if s == 0:
        @pl.when(t == 0)
        def _():
            ext_ref[slot, b, lo:pad, :] = hist_ref[b]

        @pl.when(t > 0)
        def _():
            ext_ref[slot, b, lo:pad, :] = carry_ref[c, b, lo:pad, :]
    else:
        ext_ref[slot, b, lo:pad, :] = ext_ref[1 - slot, b, lo + sr:pad + sr, :]
    ext_ref[slot, b, pad:pad + sr, :] = new_rows
    if s == n_sub - 1:
        tail = ext_ref[slot, b, lo + sr:pad + sr, :]
        carry_ref[c, b, lo:pad, :] = tail
        tc = ext_ref.shape[3]
        nh_ref[b, :, pl.ds(pl.multiple_of(c * tc, tc), tc)] = tail
    return lo


def _front_a_body(x_ref, sh_ref, sc_ref, gpre_ref, wb_ref, wc_ref, wx_ref, hist_ref, wconv_ref,
                  y_ref, nh_ref, h_ref, vec_ref, ext_ref, carry_ref, *, row0, n_sub):
    t, c = pl.program_id(1), pl.program_id(2)
    gb, rb, _ = x_ref.shape
    W = wconv_ref.shape[0]
    seq0 = row0 + pl.program_id(0) * gb

    @pl.when(c == 0)
    def _():
        _prenorm_rows(x_ref, h_ref, vec_ref, gpre_ref, sh_ref, sc_ref, seq0)

    sr = rb // n_sub
    rs = min(CONV_SUB_ROWS, sr)
    for b in range(gb):
        for s in range(n_sub):
            slot = s % 2
            h = h_ref[b * rb + s * sr:b * rb + (s + 1) * sr, :]
            zb, zc, zx = _dot(h, wb_ref[...]), _dot(h, wc_ref[...]), _dot(h, wx_ref[...])
            lo = _conv_stage(ext_ref, carry_ref, hist_ref, nh_ref, slot, b, c, t, s, n_sub, zc * zx, W)
            for r0 in range(0, sr, rs):
                acc = wconv_ref[0:1, :] * ext_ref[slot, b, lo + r0:lo + r0 + rs, :]
                for j in range(1, W):
                    acc = acc + wconv_ref[j:j + 1, :] * ext_ref[slot, b, lo + j + r0:lo + j + r0 + rs, :]
                y_ref[b, s * sr + r0:s * sr + r0 + rs, :] = (zb[r0:r0 + rs] * acc).astype(y_ref.dtype)


def _front_b_body(x_ref, sh_ref, sc_ref, gpre_ref, wa_ref, wg_ref, ba_ref, bg_ref, hist_ref, wconv_ref, bdw_ref,
                  y_ref, nh_ref, h_ref, vec_ref, ext_ref, carry_ref, ph_ref, *, row0, n_sub):
    t, c = pl.program_id(1), pl.program_id(2)
    gb, rb, _ = x_ref.shape
    W = wconv_ref.shape[0]
    seq0 = row0 + pl.program_id(0) * gb

    @pl.when(c == 0)
    def _():
        _prenorm_rows(x_ref, h_ref, vec_ref, gpre_ref, sh_ref, sc_ref, seq0)

    sr = rb // n_sub
    rs = min(CONV_SUB_ROWS, sr)
    n_shift = ph_ref.shape[2]
    for b in range(gb):
        for s in range(n_sub):
            slot = s % 2
            h = h_ref[b * rb + s * sr:b * rb + (s + 1) * sr, :]
            u = (_dot(h, wa_ref[...]) + ba_ref[...]) * jax.nn.sigmoid(_dot(h, wg_ref[...]) + bg_ref[...])
            lo = _conv_stage(ext_ref, carry_ref, hist_ref, nh_ref, slot, b, c, t, s, n_sub, u, W)
            for p in range(1, SUBLANES):
                ph_ref[slot, p - 1] = ext_ref[slot, b, p:p + n_shift, :]
            for r0 in range(0, sr, rs):
                acc = None
                for j in range(W):
                    a, p = divmod(lo + j, SUBLANES)
                    start = r0 + a * SUBLANES
                    win = (ext_ref[slot, b, start:start + rs, :] if p == 0
                           else ph_ref[slot, p - 1, start:start + rs, :])
                    term = wconv_ref[j:j + 1, :] * win
                    acc = term if acc is None else acc + term
                y_ref[b, s * sr + r0:s * sr + r0 + rs, :] = acc + bdw_ref[...]


def _mixer_front(kind, x, mod, row0, layer, g_pre, w, widx, hist, wconv, bias=None, bdw=None):
    G, R, D = x.shape
    nsplit = 3 if kind == "a" else 2
    W = wconv.shape[1]
    gb, rb = _row_blocks(G, R, FRONT_ROWS)
    n_sub = FRONT_SUB_TILES if rb % (FRONT_SUB_TILES * CONV_SUB_ROWS) == 0 else 1
    sr = rb // n_sub
    assert sr >= W - 1 and sr % min(CONV_SUB_ROWS, sr) == 0
    tc = FRONT_COLS_A if kind == "a" else FRONT_COLS_B
    nc = D // tc
    pad = -(-(W - 1) // SUBLANES) * SUBLANES
    gidx = layer * N_SUB + 1
    wspecs = [pl.BlockSpec((None, D, tc), lambda g, t, c, s=s: (widx, 0, s * nc + c)) for s in range(nsplit)]
    hist_spec = pl.BlockSpec((gb, W - 1, tc), lambda g, t, c: (g, 0, c))
    in_specs = [pl.BlockSpec((gb, rb, D), lambda g, t, c: (g, t, 0))] + _mod_specs(layer, 1, (0, 1), D) + [
        pl.BlockSpec((None, 1, D), lambda g, t, c: (gidx, 0, 0))] + wspecs
    args = [x, mod, mod, g_pre] + [w] * nsplit
    scratch = [pltpu.VMEM((gb * rb, D), BF16), pltpu.VMEM((2, ROW_BLK, D), F32),
               pltpu.VMEM((2, gb, pad + sr, tc), F32), pltpu.VMEM((nc, gb, pad, tc), F32)]
    if kind == "a":
        body, out_dtype = _front_a_body, BF16
    else:
        body, out_dtype = _front_b_body, F32
        in_specs += [pl.BlockSpec((None, 1, tc), lambda g, t, c, s=s: (widx, 0, s * nc + c)) for s in range(nsplit)]
        args += [bias.reshape(bias.shape[0], 1, nsplit * D)] * nsplit
        scratch.append(pltpu.VMEM((2, SUBLANES - 1, pad + sr - SUBLANES, tc), F32))
    in_specs += [hist_spec, pl.BlockSpec((None, W, tc), lambda g, t, c: (widx, 0, c))]
    args += [hist, wconv]
    if kind == "b":
        in_specs.append(pl.BlockSpec((None, 1, tc), lambda g, t, c: (widx, 0, c)))
        args.append(bdw.reshape(bdw.shape[0], 1, D))
    return pl.pallas_call(
        functools.partial(body, row0=row0, n_sub=n_sub),
        grid=(G // gb, R // rb, nc),
        in_specs=in_specs,
        out_specs=[pl.BlockSpec((gb, rb, tc), lambda g, t, c: (g, t, c)),
                   pl.BlockSpec((gb, W - 1, D), lambda g, t, c: (g, 0, 0))],
        out_shape=[jax.ShapeDtypeStruct((G, R, D), out_dtype),
                   jax.ShapeDtypeStruct((G, W - 1, D), F32)],
        scratch_shapes=scratch,
        compiler_params=_cparams(("arbitrary", "arbitrary", "arbitrary")),
        name="front_" + kind,
    )(*args)


def _last_visible_tile(q_pos, tk):
    return ((q_pos // CHUNK) * CHUNK + CHUNK - 1) // tk


def _lambda(lam_ref, lam_init):
    lp = lam_ref[...]
    return (jnp.exp(jnp.sum(lp[0:1] * lp[1:2], axis=-1, keepdims=True))
            - jnp.exp(jnp.sum(lp[2:3] * lp[3:4], axis=-1, keepdims=True)) + lam_init)


def _diff_out(o1, o2, lam, subg, lam_init):
    o = o1 - lam * o2
    r = o * lax.rsqrt(jnp.mean(o * o, axis=0, keepdims=True) + SUBLN_EPS) * subg
    return (r * (1.0 - lam_init)).T


def _attn_body(qi_ref, ki_ref, last_ref, slope_ref, lam_ref, subg_ref, q_ref, k_ref, v_ref, o_ref,
               m_ref, l_ref, acc_ref, ramp_ref, *, pos0, n_keys, lam_init):
    hp, s = pl.program_id(1), pl.program_id(2)
    qi, ki = qi_ref[s], ki_ref[s]
    tq = q_ref.shape[0]
    tk = k_ref.shape[0]
    dh2 = subg_ref.shape[0]
    dh = dh2 // 2
    n_heads = q_ref.shape[1] // dh2
    shift = int(math.log2(CHUNK))
    slopes = [slope_ref[hp * n_heads + hh] for hh in range(n_heads)]

    @pl.when(s == 0)
    def _():
        d = (lax.broadcasted_iota(jnp.int32, (tk, tq), 1) - lax.broadcasted_iota(jnp.int32, (tk, tq), 0)).astype(F32)
        for hh in range(n_heads):
            ramp_ref[hh] = -slopes[hh] * d

    @pl.when(ki == 0)
    def _():
        m_ref[...] = jnp.full(m_ref.shape, NEG_INF, F32)
        l_ref[...] = jnp.zeros(l_ref.shape, F32)
        acc_ref[...] = jnp.zeros(acc_ref.shape, F32)

    q_lo = pos0 + qi * tq
    k_lo = ki * tk

    def raw_scores(hh, c, l0=0):
        col = hh * dh2 + c * dh
        return lax.dot_general(k_ref[:, col:col + dh], q_ref[l0:, col:col + dh],
                               (((1,), (1,)), ((), ())), preferred_element_type=F32)

    def online_update(hh, c, t, cst, l0=0):
        i = 2 * hh + c
        m_old = m_ref[i, :, l0:]
        m_new = jnp.maximum(m_old, jnp.max(t, axis=0, keepdims=True) + cst)
        alpha = jnp.exp2(m_old - m_new)
        p = jnp.exp2(t + (cst - m_new))
        l_ref[i, :, l0:] = alpha * l_ref[i, :, l0:] + jnp.sum(p, axis=0, keepdims=True)
        pv = lax.dot_general(v_ref[:, hh * dh2:(hh + 1) * dh2], p.astype(BF16), (((0,), (0,)), ((), ())),
                             preferred_element_type=F32)
        acc_ref[i, :, l0:] = alpha * acc_ref[i, :, l0:] + pv
        m_ref[i, :, l0:] = m_new

    def masked_update(l0):
        qpos = q_lo + l0 + lax.broadcasted_iota(jnp.int32, (1, tq - l0), 1)
        kpos = k_lo + lax.broadcasted_iota(jnp.int32, (tk, 1), 0)
        visible = (jnp.right_shift(kpos, shift) <= jnp.right_shift(qpos, shift)) & (kpos < n_keys)
        dist = jnp.abs(qpos - kpos).astype(F32)
        for hh in range(n_heads):
            bias = -slopes[hh] * dist
            for c in range(2):
                online_update(hh, c, jnp.where(visible, raw_scores(hh, c, l0) + bias, NEG_INF), 0.0, l0)

    all_past = k_lo + tk - 1 < q_lo
    half = tq // 2 if (tq // 2) % LANES == 0 and tk % CHUNK == 0 else None
    late = (k_lo >= q_lo + half) if half is not None else False

    @pl.when(all_past)
    def _():
        for hh in range(n_heads):
            cst = -slopes[hh] * (q_lo - k_lo).astype(F32)
            for c in range(2):
                online_update(hh, c, raw_scores(hh, c) + ramp_ref[hh], cst)

    if half is None:
        @pl.when(jnp.logical_not(all_past))
        def _():
            masked_update(0)
    else:
        @pl.when(jnp.logical_not(all_past) & jnp.logical_not(late))
        def _():
            masked_update(0)

        @pl.when(jnp.logical_not(all_past) & late)
        def _():
            masked_update(half)

    @pl.when(last_ref[s] == 1)
    def _():
        lam = _lambda(lam_ref, lam_init)
        for hh in range(n_heads):
            o_ref[:, hh * dh2:(hh + 1) * dh2] = _diff_out(
                acc_ref[2 * hh] / l_ref[2 * hh], acc_ref[2 * hh + 1] / l_ref[2 * hh + 1], lam,
                subg_ref[...], lam_init).astype(o_ref.dtype)


def _diff_attn(q_arr, q_off, k_arr, k_off, v_arr, v_off, n_keys, pos0, lam_params, subln_g, idx, lam_init,
               slopes):
    B, Tq, _ = q_arr.shape
    Tk = k_arr.shape[1]
    dh2 = subln_g.shape[1]
    H = N_HEADS
    tq = min(ATTN_Q_TILE, Tq)
    tk = Tk if Tk <= 2 * ATTN_K_TILE + LANES else ATTN_K_TILE
    assert Tq % tq == 0 and Tk % tk == 0 and pos0 + Tq <= n_keys <= Tk
    assert 1 << int(math.log2(CHUNK)) == CHUNK
    hps = ATTN_HEADS_PER_STEP if H % ATTN_HEADS_PER_STEP == 0 else 1
    assert q_off % hps == 0 and k_off % hps == 0 and v_off % hps == 0
    nk = Tk // tk
    qis, kis, lasts = [], [], []
    for qi in range(Tq // tq):
        last = min(_last_visible_tile(pos0 + (qi + 1) * tq - 1, tk), nk - 1)
        for ki in range(last + 1):
            qis.append(qi)
            kis.append(ki)
            lasts.append(int(ki == last))
    sched = [jnp.asarray(np.asarray(a, np.int32)) for a in (qis, kis, lasts)]

    body = functools.partial(_attn_body, pos0=pos0, n_keys=n_keys, lam_init=lam_init)
    grid_spec = pltpu.PrefetchScalarGridSpec(
        num_scalar_prefetch=3,
        grid=(B, H // hps, len(qis)),
        in_specs=[pl.BlockSpec(memory_space=pltpu.SMEM),
                  pl.BlockSpec((None, 4, dh2 // 2), lambda b, h, s, qi, ki, la: (idx, 0, 0)),
                  pl.BlockSpec((None, dh2, 1), lambda b, h, s, qi, ki, la: (idx, 0, 0)),
                  pl.BlockSpec((None, tq, hps * dh2), lambda b, h, s, qi, ki, la: (b, qi[s], q_off // hps + h)),
                  pl.BlockSpec((None, tk, hps * dh2), lambda b, h, s, qi, ki, la: (b, ki[s], k_off // hps + h)),
                  pl.BlockSpec((None, tk, hps * dh2), lambda b, h, s, qi, ki, la: (b, ki[s], v_off // hps + h))],
        out_specs=pl.BlockSpec((None, tq, hps * dh2), lambda b, h, s, qi, ki, la: (b, qi[s], h)),
        scratch_shapes=[pltpu.VMEM((2 * hps, 1, tq), F32), pltpu.VMEM((2 * hps, 1, tq), F32),
                        pltpu.VMEM((2 * hps, dh2, tq), F32), pltpu.VMEM((hps, tk, tq), F32)])
    return pl.pallas_call(
        body,
        grid_spec=grid_spec,
        out_shape=jax.ShapeDtypeStruct((B, Tq, H * dh2), BF16),
        compiler_params=_cparams(("arbitrary", "arbitrary", "arbitrary")),
        name="diff_attn",
    )(*sched, slopes * LOG2E, lam_params, subln_g.reshape(subln_g.shape[0], dh2, 1), q_arr, k_arr, v_arr)


def _cached_attn_body(slope_ref, lam_ref, subg_ref, qkv_ref, ck1_ref, ck2_ref, cv1_ref, cv2_ref, o_ref,
                      *, past, lam_init):
    R = qkv_ref.shape[0]
    dh = ck1_ref.shape[1]
    dh2 = 2 * dh
    H = ck1_ref.shape[0] // past
    D = H * dh2
    shift = int(math.log2(CHUNK))
    trans_b = (((1,), (1,)), ((), ()))

    d_cache = (past + lax.broadcasted_iota(jnp.int32, (R, past), 0)
               - lax.broadcasted_iota(jnp.int32, (R, past), 1)).astype(F32)
    qpos = past + lax.broadcasted_iota(jnp.int32, (R, 1), 0)
    kpos = past + lax.broadcasted_iota(jnp.int32, (1, R), 1)
    vis_new = jnp.right_shift(kpos, shift) <= jnp.right_shift(qpos, shift)
    d_new = jnp.abs(qpos - kpos).astype(F32)
    lam = _lambda(lam_ref, lam_init)

    for h in range(H):
        slope = slope_ref[h]
        head_rows = pl.ds(h, past, stride=H)
        kc = (ck1_ref[head_rows, :].astype(BF16), ck2_ref[head_rows, :].astype(BF16))
        vc = jnp.concatenate([cv1_ref[head_rows, :].astype(BF16), cv2_ref[head_rows, :].astype(BF16)], axis=-1)
        q = qkv_ref[:, h * dh2:(h + 1) * dh2]
        kn = qkv_ref[:, D + h * dh2:D + (h + 1) * dh2]
        vn = qkv_ref[:, 2 * D + h * dh2:2 * D + (h + 1) * dh2]
        outs = []
        for c in range(2):
            qc = q[:, c * dh:(c + 1) * dh]
            t1 = lax.dot_general(qc, kc[c], trans_b, preferred_element_type=F32) - slope * d_cache
            t2 = lax.dot_general(qc, kn[:, c * dh:(c + 1) * dh], trans_b, preferred_element_type=F32) - slope * d_new
            t2 = jnp.where(vis_new, t2, NEG_INF)
            m = jnp.maximum(jnp.max(t1, axis=-1, keepdims=True), jnp.max(t2, axis=-1, keepdims=True))
            p1 = jnp.exp2(t1 - m)
            p2 = jnp.exp2(t2 - m)
            l = jnp.sum(p1, axis=-1, keepdims=True) + jnp.sum(p2, axis=-1, keepdims=True)
            outs.append((_dot(p1.astype(BF16), vc) + _dot(p2.astype(BF16), vn)) / l)
        o = outs[0] - lam * outs[1]
        r = o * lax.rsqrt(jnp.mean(o * o, axis=-1, keepdims=True) + SUBLN_EPS) * subg_ref[...]
        o_ref[:, h * dh2:(h + 1) * dh2] = (r * (1.0 - lam_init)).astype(o_ref.dtype)


def _cached_attn(qkv, cache_k, cache_v, idx, lam_params, subln_g, lam_init, slopes):
    B, R, _ = qkv.shape
    n_c, _, past, H, dh2 = cache_k.shape
    D = H * dh2
    body = functools.partial(_cached_attn_body, past=past, lam_init=lam_init)
    cspecs = [pl.BlockSpec((None, None, past * H, dh2 // 2), lambda b, half=half: (idx, b, 0, half))
              for half in (0, 1)]
    return pl.pallas_call(
        body,
        grid=(B,),
        in_specs=[pl.BlockSpec(memory_space=pltpu.SMEM),
                  pl.BlockSpec((None, 4, dh2 // 2), lambda b: (idx, 0, 0)),
                  pl.BlockSpec((None, 1, dh2), lambda b: (idx, 0, 0)),
                  pl.BlockSpec((None, R, 3 * D), lambda b: (b, 0, 0)),
                  ] + cspecs + cspecs,
        out_specs=pl.BlockSpec((None, R, D), lambda b: (b, 0, 0)),
        out_shape=jax.ShapeDtypeStruct((B, R, D), BF16),
        compiler_params=_cparams(("arbitrary",)),
        name="cached_attn",
    )(slopes * LOG2E, lam_params, subln_g.reshape(subln_g.shape[0], 1, dh2), qkv,
      *([cache_k.reshape(n_c, B, past * H, dh2)] * 2 + [cache_v.reshape(n_c, B, past * H, dh2)] * 2))


def _trunk(x, row0, mod, hist_a, hist_b, hist_k, hist_v, pos0, p, ffn_weights):
    G, R, D = x.shape
    depth = p["depth"]
    slopes = jnp.asarray([2.0 ** (-8.0 * (h + 1) / N_HEADS) for h in range(N_HEADS)], dtype=F32)
    new_a, new_b, new_k, new_v, copies = [], [], [], [], {}
    for i in range(depth):
        x, copies[i, 0] = _ffn(x, mod, row0, i, 0, p["g_pre"], p["g_post"], ffn_weights[i, 0])
        kind, idx = i % N_MIXERS, i // N_MIXERS
        if kind == 0:
            y, st = _mixer_front("a", x, mod, row0, i, p["g_pre"], p["sc_w_in"], idx, hist_a[idx], p["sc_w_conv"])
            new_a.append(st)
            x = _mm_post(y, x, mod, row0, i, p["g_post"], p["sc_w_out"], idx)
        elif kind == 1:
            y, st = _mixer_front("b", x, mod, row0, i, p["g_pre"], p["cm_w_pw1"], idx, hist_b[idx], p["cm_w_dw"],
                                 bias=p["cm_b_pw1"], bdw=p["cm_b_dw"])
            new_b.append(st)
            x = _mm_post(y, x, mod, row0, i, p["g_post"], p["cm_w_pw2"], idx, bias=p["cm_b_pw2"],
                         ln=(p["cm_ln_g"], p["cm_ln_b"]))
        else:
            lam_init = 0.8 - 0.6 * math.exp(-0.3 * i)
            dh2 = p["attn_subln_g"].shape[1]
            qkv, k_rows, v_rows = _qkv_proj(x, mod, row0, i, p["g_pre"], p["attn_w_qkv"], idx,
                                            (dh2 // 2) ** -0.5 * LOG2E)
            new_k.append(k_rows)
            new_v.append(v_rows)
            nblk = D // dh2
            if hist_k is None:
                o = _diff_attn(qkv, 0, qkv, nblk, qkv, 2 * nblk, R, pos0, p["attn_lambda"], p["attn_subln_g"],
                               idx, lam_init, slopes)
            else:
                assert pos0 == hist_k.shape[2]
                o = _cached_attn(qkv, hist_k, hist_v, idx, p["attn_lambda"], p["attn_subln_g"], lam_init, slopes)
            x = _mm_post(o, x, mod, row0, i, p["g_post"], p["attn_w_o"], idx)
        x, copies[i, 1] = _ffn(x, mod, row0, i, 2, p["g_pre"], p["g_post"], ffn_weights[i, 1])
    return (x, jnp.stack(new_a), jnp.stack(new_b), jnp.stack(new_k), jnp.stack(new_v)), copies


def kernel(x_prompt, x_sample, state_conv_a, state_conv_b, cache_k, cache_v, c_prompt, c_sample, w_mod, b_mod, g_pre, g_post, ffn_w_gu, ffn_w_down, sc_w_in, sc_w_conv, sc_w_out, cm_w_pw1, cm_b_pw1, cm_w_dw, cm_b_dw, cm_ln_g, cm_ln_b, cm_w_pw2, cm_b_pw2, attn_w_qkv, attn_lambda, attn_subln_g, attn_w_o):
    Bp, T, D = x_prompt.shape
    Bs = x_sample.shape[0]
    depth = g_pre.shape[0]
    n_a, n_b = state_conv_a.shape[0], state_conv_b.shape[0]
    assert Bp + Bs <= SEQ_PAD
    H, dh2 = N_HEADS, attn_subln_g.shape[1]

    c_all = jnp.concatenate([c_prompt, c_sample, jnp.zeros((SEQ_PAD - Bp - Bs, D), F32)], axis=0)
    mod = _adaln_mod(c_all, w_mod, b_mod)

    p = dict(
        depth=depth,
        g_pre=g_pre.reshape(depth * N_SUB, 1, D), g_post=g_post.reshape(depth * N_SUB, 1, D),
        sc_w_in=sc_w_in.astype(BF16), sc_w_conv=sc_w_conv, sc_w_out=sc_w_out.astype(BF16),
        cm_w_pw1=cm_w_pw1.astype(BF16), cm_b_pw1=cm_b_pw1, cm_w_dw=cm_w_dw, cm_b_dw=cm_b_dw,
        cm_ln_g=cm_ln_g, cm_ln_b=cm_ln_b, cm_w_pw2=cm_w_pw2.astype(BF16), cm_b_pw2=cm_b_pw2,
        attn_w_qkv=attn_w_qkv.astype(BF16), attn_lambda=attn_lambda, attn_subln_g=attn_subln_g,
        attn_w_o=attn_w_o.astype(BF16))

    past = cache_k.shape[2]
    f32_weights = {(i, f): ("f32", ffn_w_gu, ffn_w_down, f) for i in range(depth) for f in range(2)}
    (y_s, a_s, b_s, k_s, v_s), copies = _trunk(x_sample, Bp, mod, state_conv_a, state_conv_b, cache_k, cache_v,
                                               past, p, f32_weights)
    bf16_weights = {key: ("bf16",) + copies[key] for key in copies}
    zeros_a = jnp.zeros((n_a, Bp) + state_conv_a.shape[2:], F32)
    zeros_b = jnp.zeros((n_b, Bp) + state_conv_b.shape[2:], F32)
    (y_p, a_p, b_p, k_p, v_p), _ = _trunk(x_prompt, 0, mod, zeros_a, zeros_b, None, None, 0, p, bf16_weights)

    def heads(a):
        return a.reshape(a.shape[:3] + (H, dh2))
    return (y_p, y_s, a_p, a_s, b_p, b_s, heads(k_p), heads(v_p), heads(k_s), heads(v_s))
```
